```python
import jax, jax.numpy as jnp
from jax import lax
import numpy as np

D_MODEL = 1024
BATCH = 16
SEQ = 2048
DEPTH = 1

D_INNER = 2 * D_MODEL
GLA_WIDTH = D_INNER // 2
MLSTM_WIDTH = D_INNER - GLA_WIDTH
GLA_HEADS = 4
GLA_KEY_WIDTH = GLA_WIDTH // 2
GLA_DK = GLA_KEY_WIDTH // GLA_HEADS
GLA_DV = GLA_WIDTH // GLA_HEADS
GLA_GATE_RANK = 16
GLA_GATE_TAU = 16.0
MLSTM_HEADS = 4
MLSTM_DH = MLSTM_WIDTH // MLSTM_HEADS
QKV_BLOCK = 4
QKV_NBLOCKS = MLSTM_WIDTH // QKV_BLOCK
CONV_WIDTH = 4
CHUNK = 64
EPS = 1e-6
SPLIT_SIZES = (GLA_KEY_WIDTH, GLA_KEY_WIDTH, GLA_WIDTH, GLA_WIDTH, GLA_GATE_RANK, MLSTM_WIDTH, MLSTM_WIDTH)
PROJ_WIDTH = sum(SPLIT_SIZES)

kernel_name = "hymba_style_gla_mlstm_hybrid"


def rms_norm(x, g):
    xf = x.astype(jnp.float32)
    y = xf * lax.rsqrt(jnp.mean(xf * xf, axis=-1, keepdims=True) + EPS)
    return (y * g.astype(jnp.float32)).astype(x.dtype)


def to_chunks(t):
    b, s = t.shape[0], t.shape[1]
    t = t.reshape((b, s // CHUNK, CHUNK) + t.shape[2:])
    if t.ndim == 5:
        return t.transpose(1, 0, 3, 2, 4)
    return t.transpose(1, 0, 3, 2)


def from_chunks(t):
    n, b, h, c, d = t.shape
    return t.transpose(1, 0, 3, 2, 4).reshape(b, n * c, h, d)


def gla_chunked(q, k, v, log_a):
    bsz = q.shape[0]
    qc = to_chunks(q.astype(jnp.float32) * (GLA_DK ** -0.5))
    kc = to_chunks(k.astype(jnp.float32))
    vc = to_chunks(v.astype(jnp.float32))
    bc = jnp.cumsum(to_chunks(log_a.astype(jnp.float32)), axis=3)
    causal = jnp.tril(jnp.ones((CHUNK, CHUNK), dtype=bool))

    def step(state, inp):
        q_, k_, v_, b_ = inp
        diff = b_[:, :, :, None, :] - b_[:, :, None, :, :]
        decay = jnp.exp(jnp.where(causal[:, :, None], diff, -jnp.inf))
        scores = jnp.einsum('bhid,bhjd,bhijd->bhij', q_, k_, decay)
        o = jnp.einsum('bhij,bhje->bhie', scores, v_) + \
            jnp.einsum('bhid,bhde->bhie', q_ * jnp.exp(b_), state)
        b_last = b_[:, :, -1:, :]
        k_dec = k_ * jnp.exp(b_last - b_)
        state = jnp.exp(b_last[:, :, 0, :])[..., None] * state + \
            jnp.einsum('bhjd,bhje->bhde', k_dec, v_)
        return state, o

    state0 = jnp.zeros((bsz, GLA_HEADS, GLA_DK, GLA_DV), jnp.float32)
    _, o = lax.scan(step, state0, (qc, kc, vc, bc))
    return from_chunks(o)


def mlstm_chunked(q, k, v, i_pre, f_pre):
    bsz = q.shape[0]
    qc = to_chunks(q.astype(jnp.float32))
    kc = to_chunks(k.astype(jnp.float32) * (MLSTM_DH ** -0.5))
    vc = to_chunks(v.astype(jnp.float32))
    ic = to_chunks(i_pre.astype(jnp.float32))
    bc = jnp.cumsum(to_chunks(jax.nn.log_sigmoid(f_pre.astype(jnp.float32))), axis=-1)
    causal = jnp.tril(jnp.ones((CHUNK, CHUNK), dtype=bool))

    def step(carry, inp):
        c_hat, n_hat, m_prev = carry
        q_, k_, v_, b_, i_ = inp
        log_d = b_[..., :, None] - b_[..., None, :] + i_[..., None, :]
        log_d = jnp.where(causal, log_d, -jnp.inf)
        log_inter = b_ + m_prev[..., None]
        m = jnp.maximum(log_inter, jnp.max(log_d, axis=-1))
        s = jnp.einsum('bhid,bhjd->bhij', q_, k_) * jnp.exp(log_d - m[..., None])
        w_inter = jnp.exp(log_inter - m)
        num = jnp.einsum('bhij,bhje->bhie', s, v_) + \
            w_inter[..., None] * jnp.einsum('bhid,bhed->bhie', q_, c_hat)
        den = jnp.sum(s, axis=-1) + w_inter * jnp.einsum('bhid,bhd->bhi', q_, n_hat)
        h = num / jnp.maximum(jnp.abs(den), jnp.exp(-m))[..., None]
        m_new = m[..., -1]
        decay_prev = jnp.exp(b_[..., -1] + m_prev - m_new)
        wk = jnp.exp(b_[..., -1:] - b_ + i_ - m_new[..., None])
        c_new = decay_prev[..., None, None] * c_hat + jnp.einsum('bhj,bhje,bhjd->bhed', wk, v_, k_)
        n_new = decay_prev[..., None] * n_hat + jnp.einsum('bhj,bhjd->bhd', wk, k_)
        return (c_new, n_new, m_new), h

    carry0 = (jnp.zeros((bsz, MLSTM_HEADS, MLSTM_DH, MLSTM_DH), jnp.float32),
              jnp.zeros((bsz, MLSTM_HEADS, MLSTM_DH), jnp.float32),
              jnp.zeros((bsz, MLSTM_HEADS), jnp.float32))
    _, h = lax.scan(step, carry0, (qc, kc, vc, bc, ic))
    return from_chunks(h)


def headwise(t, w):
    b, s, _ = t.shape
    tb = t.reshape(b, s, QKV_NBLOCKS, QKV_BLOCK)
    return jnp.einsum('bsnd,nde->bsne', tb, w).reshape(b, s, MLSTM_WIDTH)


def head_rms_norm(t, g):
    return rms_norm(t, g)


def setup_inputs(seed: int = 0) -> dict:
    key = jax.random.key(seed)
    ks = jax.random.split(key, 24)
    f32 = jnp.float32
    nrm = lambda k, shape, scale: jax.random.normal(k, shape, f32) * scale
    gain = lambda k, shape: 1.0 + 0.02 * jax.random.normal(k, shape, f32)
    return {
        "x": jax.random.normal(ks[0], (BATCH, SEQ, D_MODEL), f32),
        "norm_g": gain(ks[1], (D_MODEL,)),
        "w_in": nrm(ks[2], (D_MODEL, PROJ_WIDTH), D_MODEL ** -0.5),
        "w_gla_gate_up": nrm(ks[3], (GLA_GATE_RANK, GLA_KEY_WIDTH), GLA_GATE_RANK ** -0.5),
        "b_gla_gate": nrm(ks[4], (GLA_KEY_WIDTH,), 0.1),
        "gla_norm_g": gain(ks[5], (GLA_HEADS, GLA_DV)),
        "conv_w": nrm(ks[6], (CONV_WIDTH, MLSTM_WIDTH), CONV_WIDTH ** -0.5),
        "conv_b": nrm(ks[7], (MLSTM_WIDTH,), 0.02),
        "w_q_m": nrm(ks[8], (QKV_NBLOCKS, QKV_BLOCK, QKV_BLOCK), QKV_BLOCK ** -0.5),
        "w_k_m": nrm(ks[9], (QKV_NBLOCKS, QKV_BLOCK, QKV_BLOCK), QKV_BLOCK ** -0.5),
        "w_v_m": nrm(ks[10], (QKV_NBLOCKS, QKV_BLOCK, QKV_BLOCK), QKV_BLOCK ** -0.5),
        "w_igate": nrm(ks[11], (3 * MLSTM_WIDTH, MLSTM_HEADS), (3 * MLSTM_WIDTH) ** -0.5),
        "b_igate": nrm(ks[12], (MLSTM_HEADS,), 0.1),
        "w_fgate": nrm(ks[13], (3 * MLSTM_WIDTH, MLSTM_HEADS), (3 * MLSTM_WIDTH) ** -0.5),
        "b_fgate": jnp.linspace(3.0, 6.0, MLSTM_HEADS, dtype=f32) + nrm(ks[14], (MLSTM_HEADS,), 0.01),
        "mlstm_norm_g": gain(ks[15], (MLSTM_HEADS, MLSTM_DH)),
        "mlstm_skip": gain(ks[16], (MLSTM_WIDTH,)),
        "w_out": nrm(ks[17], (D_INNER, D_MODEL), D_INNER ** -0.5),
        "final_norm_g": gain(ks[18], (D_MODEL,)),
    }


def reference(x, norm_g, w_in, w_gla_gate_up, b_gla_gate, gla_norm_g, conv_w, conv_b,
              w_q_m, w_k_m, w_v_m, w_igate, b_igate, w_fgate, b_fgate,
              mlstm_norm_g, mlstm_skip, w_out, final_norm_g):
    bsz, seq, _ = x.shape
    for _layer in range(DEPTH):
        u = rms_norm(x, norm_g)
        proj = u @ w_in
        cuts = list(np.cumsum(SPLIT_SIZES)[:-1])
        q_g, k_g, v_g, z_g, r_g, x_m, z_m = jnp.split(proj, cuts, axis=-1)

        log_a = jax.nn.log_sigmoid((r_g @ w_gla_gate_up + b_gla_gate).astype(jnp.float32)) / GLA_GATE_TAU
        o_g = gla_chunked(q_g.reshape(bsz, seq, GLA_HEADS, GLA_DK),
                          k_g.reshape(bsz, seq, GLA_HEADS, GLA_DK),
                          v_g.reshape(bsz, seq, GLA_HEADS, GLA_DV),
                          log_a.reshape(bsz, seq, GLA_HEADS, GLA_DK))
        o_g = head_rms_norm(o_g, gla_norm_g).reshape(bsz, seq, GLA_WIDTH).astype(x.dtype)
        o_g = o_g * jax.nn.silu(z_g)

        conv = lax.conv_general_dilated(
            x_m, conv_w[:, None, :].astype(x_m.dtype), window_strides=(1,),
            padding=[(CONV_WIDTH - 1, 0)], dimension_numbers=('NWC', 'WIO', 'NWC'),
            feature_group_count=MLSTM_WIDTH)
        c_act = jax.nn.silu(conv + conv_b)
        q_m = headwise(c_act, w_q_m)
        k_m = headwise(c_act, w_k_m)
        v_m = headwise(x_m, w_v_m)
        qkv = jnp.concatenate([q_m, k_m, v_m], axis=-1)
        i_pre = qkv @ w_igate + b_igate
        f_pre = qkv @ w_fgate + b_fgate
        h_m = mlstm_chunked(q_m.reshape(bsz, seq, MLSTM_HEADS, MLSTM_DH),
                            k_m.reshape(bsz, seq, MLSTM_HEADS, MLSTM_DH),
                            v_m.reshape(bsz, seq, MLSTM_HEADS, MLSTM_DH),
                            i_pre, f_pre)
        h_m = head_rms_norm(h_m, mlstm_norm_g).reshape(bsz, seq, MLSTM_WIDTH).astype(x.dtype)
        o_m = (h_m + mlstm_skip * c_act) * jax.nn.silu(z_m)

        x = x + jnp.concatenate([o_g, o_m], axis=-1) @ w_out
    return rms_norm(x, final_norm_g)
```

```python
import jax
import jax.numpy as jnp
from jax import lax
from jax.experimental import pallas as pl
from jax.experimental.pallas import tpu as pltpu

F32 = jnp.float32
BF16 = jnp.bfloat16

D_MODEL = 1024
GLA_HEADS = 4
GLA_DK = 128
GLA_DV = 256
GLA_KEY_WIDTH = GLA_HEADS * GLA_DK
GLA_WIDTH = GLA_HEADS * GLA_DV
GLA_GATE_RANK = 16
GLA_GATE_TAU = 16.0
MLSTM_HEADS = 4
MLSTM_DH = 256
MLSTM_WIDTH = MLSTM_HEADS * MLSTM_DH
QKV_BLOCK = 4
CONV_WIDTH = 4
EPS = 1e-6

LANES = 128
SUBLANES = 8
TS = 256
CG = 128
GLA_LEVELS = 7
HIST = SUBLANES
VMEM_LIMIT_BYTES = 60 * 1024 * 1024


def _dot(a, b):
    return jnp.dot(a, b, preferred_element_type=F32)


def _dot_nt(a, b):
    return lax.dot_general(a, b, (((1,), (1,)), ((), ())), preferred_element_type=F32)


def _dot_tn(a, b):
    return lax.dot_general(a, b, (((0,), (0,)), ((), ())), preferred_element_type=F32)


def _log_sigmoid(z):
    return jnp.minimum(z, 0.0) - jnp.log1p(jnp.exp(-jnp.abs(z)))


def _silu(z):
    return z * (1.0 / (1.0 + jnp.exp(-z)))


def _split3(x):
    h1 = x.astype(BF16)
    r1 = x - h1.astype(F32)
    h2 = r1.astype(BF16)
    h3 = (r1 - h2.astype(F32)).astype(BF16)
    return h1, h2, h3


def _cumsum_rows(tri, x):
    h1, h2, h3 = _split3(x)
    return _dot(tri, h1) + _dot(tri, h2) + _dot(tri, h3)


def _tri(n):
    row = lax.broadcasted_iota(jnp.int32, (n, n), 0)
    col = lax.broadcasted_iota(jnp.int32, (n, n), 1)
    return jnp.where(row >= col, 1.0, 0.0).astype(BF16), row >= col


def _pair_level(n, levels):
    row = lax.broadcasted_iota(jnp.int32, (n, n), 0)
    col = lax.broadcasted_iota(jnp.int32, (n, n), 1)
    x = jnp.bitwise_xor(row, col)
    lvl = jnp.full((n, n), -1, jnp.int32)
    for p in range(levels):
        lvl = jnp.where(jnp.right_shift(x, p) == 1, p, lvl)
    return jnp.where(row > col, lvl, -1)


def _midpoint_rows(b_ref, lanes, p):
    s = 1 << (p + 1)
    half = s // 2
    if s >= SUBLANES:
        pieces = []
        for g in range(CG // s):
            row = g * s + half - 1
            pieces.append(jnp.broadcast_to(b_ref[row:row + 1, lanes], (s, LANES)))
        return pieces[0] if len(pieces) == 1 else jnp.concatenate(pieces, axis=0)
    assert s == 4
    sub = lax.broadcasted_iota(jnp.int32, (SUBLANES, LANES), 0)
    pieces = []
    for g in range(CG // SUBLANES):
        lo = jnp.broadcast_to(b_ref[g * 8 + 1:g * 8 + 2, lanes], (SUBLANES, LANES))
        hi = jnp.broadcast_to(b_ref[g * 8 + 5:g * 8 + 6, lanes], (SUBLANES, LANES))
        pieces.append(jnp.where(sub < 4, lo, hi))
    return jnp.concatenate(pieces, axis=0)


def _layer_kernel(x_ref, ng_ref, wmain_ref, wr_ref, wup_ref, bg_ref, glag_ref, cw_ref, cb_ref,
                  bdqk_ref, bdv_ref, wgate_ref, bgate_ref, mng_ref, skip_ref, wout_ref, fng_ref,
                  out_ref,
                  u_s, qg_s, kg_s, vg_s, zg_s, la_s, b_s, xm_s, zm_s, cact_s, qm_s, km_s, vm_s,
                  mix_s, sg_s, cm_s, nm_s, mm_s):
    t = pl.program_id(1)

    @pl.when(t == 0)
    def _reset_state():
        sg_s[...] = jnp.zeros_like(sg_s)
        cm_s[...] = jnp.zeros_like(cm_s)
        nm_s[...] = jnp.zeros_like(nm_s)
        mm_s[...] = jnp.zeros_like(mm_s)
        xm_s[0:HIST, :] = jnp.zeros((HIST, MLSTM_WIDTH), F32)

    x = x_ref[...]
    ms = jnp.mean(x * x, axis=-1, keepdims=True)
    u_s[...] = (x * lax.rsqrt(ms + EPS) * ng_ref[...]).astype(BF16)
    u = u_s[...]
    c0 = 0
    qg_s[...] = _dot(u, wmain_ref[:, c0:c0 + GLA_KEY_WIDTH]) * (GLA_DK ** -0.5)
    c0 += GLA_KEY_WIDTH
    kg_s[...] = _dot(u, wmain_ref[:, c0:c0 + GLA_KEY_WIDTH])
    c0 += GLA_KEY_WIDTH
    vg_s[...] = _dot(u, wmain_ref[:, c0:c0 + GLA_WIDTH]).astype(BF16)
    c0 += GLA_WIDTH
    zg_s[...] = _dot(u, wmain_ref[:, c0:c0 + GLA_WIDTH])
    c0 += GLA_WIDTH
    xm_s[HIST:HIST + TS, :] = _dot(u, wmain_ref[:, c0:c0 + MLSTM_WIDTH])
    c0 += MLSTM_WIDTH
    zm_s[...] = _dot(u, wmain_ref[:, c0:c0 + MLSTM_WIDTH])
    r_g = _dot(u, wr_ref[...]).astype(BF16)
    gate = _dot(r_g, wup_ref[...]) + bg_ref[...]
    la_s[...] = _log_sigmoid(gate) * (1.0 / GLA_GATE_TAU)

    tri_g, _ = _tri(CG)
    lvl = _pair_level(CG, GLA_LEVELS)
    for c in range(TS // CG):
        rows = slice(c * CG, (c + 1) * CG)
        b_s[...] = _cumsum_rows(tri_g, la_s[rows, :])
        for h in range(GLA_HEADS):
            kl = slice(h * GLA_DK, (h + 1) * GLA_DK)
            vl = slice(h * GLA_DV, (h + 1) * GLA_DV)
            q = qg_s[rows, kl]
            k = kg_s[rows, kl]
            b = b_s[:, kl]
            e0 = jnp.exp(la_s[rows, kl])
            sc = jnp.where(lvl == 0, _dot_nt((q * e0).astype(BF16), k.astype(BF16)), 0.0)
            for p in range(1, GLA_LEVELS):
                e = jnp.exp(-jnp.abs(b - _midpoint_rows(b_s, kl, p)))
                pr = _dot_nt((q * e).astype(BF16), (k * e).astype(BF16))
                sc = jnp.where(lvl == p, pr, sc)
            v = vg_s[rows, vl]
            st = sg_s[h]
            diag = jnp.sum(q * k, axis=-1, keepdims=True)
            o = (_dot(sc.astype(BF16), v) + diag * v.astype(F32)
                 + _dot_nt((q * jnp.exp(b)).astype(BF16), st.astype(BF16)))
            b_last = b[CG - 1:CG, :]
            k_dec = (k * jnp.exp(b_last - b)).astype(BF16)
            sg_s[h] = st * jnp.exp(b_last) + _dot_tn(v, k_dec)
            o = o * lax.rsqrt(jnp.mean(o * o, axis=-1, keepdims=True) + EPS) * glag_ref[:, vl]
            mix_s[rows, vl] = (o * _silu(zg_s[rows, vl])).astype(BF16)

    conv = cb_ref[...] + cw_ref[3:4, :] * xm_s[HIST:HIST + TS, :]
    for w in range(CONV_WIDTH - 1):
        off = HIST - (CONV_WIDTH - 1) + w
        conv = conv + cw_ref[w:w + 1, :] * xm_s[off:off + TS, :]
    cact_s[...] = _silu(conv)
    for j in range(MLSTM_WIDTH // 256):
        cl = slice(j * 256, (j + 1) * 256)
        qk = _dot(cact_s[:, cl].astype(BF16), bdqk_ref[j])
        qm_s[:, cl] = qk[:, 0:256].astype(BF16)
        km_s[:, cl] = qk[:, 256:512].astype(BF16)
        vm_s[:, cl] = _dot(xm_s[HIST:HIST + TS, cl].astype(BF16), bdv_ref[j]).astype(BF16)
    xm_s[0:HIST, :] = xm_s[TS:TS + HIST, :]
    g = (_dot(qm_s[...], wgate_ref[0:MLSTM_WIDTH, :])
         + _dot(km_s[...], wgate_ref[MLSTM_WIDTH:2 * MLSTM_WIDTH, :])
         + _dot(vm_s[...], wgate_ref[2 * MLSTM_WIDTH:3 * MLSTM_WIDTH, :])
         + bgate_ref[...])
    tri_m, causal = _tri(TS)
    bcum = _cumsum_rows(tri_m, _log_sigmoid(g))
    bcum = pltpu.roll(bcum, LANES - MLSTM_HEADS, axis=1)
    a_all = g - bcum
    a_all_t = a_all.T

    for h in range(MLSTM_HEADS):
        hl = slice(h * MLSTM_DH, (h + 1) * MLSTM_DH)
        a_row = a_all_t[h:h + 1, :]
        a_col = a_all[:, h:h + 1]
        b_col = bcum[:, h:h + 1]
        m_prev = mm_s[h][0:1, 0:1]
        run_max = jnp.max(jnp.where(causal, a_row, -jnp.inf), axis=-1, keepdims=True)
        m_rel = jnp.maximum(m_prev, run_max)
        dmat = jnp.exp(jnp.where(causal, a_row - m_rel, -jnp.inf))
        q = qm_s[:, hl]
        k = (km_s[:, hl].astype(F32) * (MLSTM_DH ** -0.5)).astype(BF16)
        v = vm_s[:, hl]
        s = _dot_nt(q, k) * dmat
        w_inter = jnp.exp(m_prev - m_rel)
        ct = cm_s[h]
        n_row = nm_s[h][0:1, :]
        num = _dot(s.astype(BF16), v) + w_inter * _dot(q, ct.astype(BF16))
        den = (jnp.sum(s, axis=-1, keepdims=True)
               + w_inter * jnp.sum(q.astype(F32) * n_row, axis=-1, keepdims=True))
        m_abs = b_col + m_rel
        hh = num * (1.0 / jnp.maximum(jnp.abs(den), jnp.exp(-m_abs)))
        hh = hh * lax.rsqrt(jnp.mean(hh * hh, axis=-1, keepdims=True) + EPS) * mng_ref[:, hl]
        o = (hh + skip_ref[:, hl] * cact_s[:, hl]) * _silu(zm_s[:, hl])
        mix_s[:, GLA_WIDTH + h * MLSTM_DH:GLA_WIDTH + (h + 1) * MLSTM_DH] = o.astype(BF16)
        m_last = m_rel[TS - 1:TS, :]
        decay = jnp.exp(m_prev - m_last)
        wk_col = jnp.exp(a_col - m_last)
        wk_row = jnp.exp(a_row - m_last)
        vw = (v.astype(F32) * wk_col).astype(BF16)
        cm_s[h] = decay * ct + _dot_tn(k, vw)
        n_add = _dot(jnp.broadcast_to(wk_row, (SUBLANES, TS)).astype(BF16), k)
        nm_s[h] = decay * nm_s[h] + n_add
        mm_s[h] = jnp.broadcast_to(b_col[TS - 1:TS, :] + m_last, (SUBLANES, LANES))

    y = x_ref[...] + _dot(mix_s[...], wout_ref[...])
    out_ref[...] = y * lax.rsqrt(jnp.mean(y * y, axis=-1, keepdims=True) + EPS) * fng_ref[...]


def _block_diag_tiles(w):
    per_tile = 256 // QKV_BLOCK
    wt = w.reshape(-1, per_tile, QKV_BLOCK, QKV_BLOCK)
    eye = jnp.eye(per_tile, dtype=w.dtype)
    return jnp.einsum('tmde,mn->tmdne', wt, eye).reshape(-1, 256, 256)


def kernel(x, norm_g, w_in, w_gla_gate_up, b_gla_gate, gla_norm_g, conv_w, conv_b, w_q_m, w_k_m, w_v_m, w_igate, b_igate, w_fgate, b_fgate, mlstm_norm_g, mlstm_skip, w_out, final_norm_g):
    bsz, seq, d = x.shape
    assert d == D_MODEL and seq % TS == 0

    r0 = 2 * GLA_KEY_WIDTH + 2 * GLA_WIDTH
    w_main = jnp.concatenate([w_in[:, :r0], w_in[:, r0 + GLA_GATE_RANK:]], axis=1).astype(BF16)
    w_r = jnp.pad(w_in[:, r0:r0 + GLA_GATE_RANK], ((0, 0), (0, LANES - GLA_GATE_RANK))).astype(BF16)
    w_up = jnp.pad(w_gla_gate_up, ((0, LANES - GLA_GATE_RANK), (0, 0))).astype(BF16)
    bd_qk = jnp.concatenate([_block_diag_tiles(w_q_m), _block_diag_tiles(w_k_m)], axis=2).astype(BF16)
    bd_v = _block_diag_tiles(w_v_m).astype(BF16)
    w_gate = jnp.pad(jnp.concatenate([w_igate, w_fgate], axis=1),
                     ((0, 0), (0, LANES - 2 * MLSTM_HEADS))).astype(BF16)
    b_gate = jnp.pad(jnp.concatenate([b_igate, b_fgate]), (0, LANES - 2 * MLSTM_HEADS)).reshape(1, LANES)
    conv_w8 = jnp.pad(conv_w, ((0, SUBLANES - CONV_WIDTH), (0, 0)))
    row = lambda a: a.reshape(1, -1)

    operands = [
        x, row(norm_g), w_main, w_r, w_up, row(b_gla_gate), row(gla_norm_g), conv_w8, row(conv_b),
        bd_qk, bd_v, w_gate, b_gate, row(mlstm_norm_g), row(mlstm_skip), w_out.astype(BF16),
        row(final_norm_g),
    ]

    def const_spec(a):
        nd = a.ndim
        return pl.BlockSpec(a.shape, lambda b, t, _nd=nd: (0,) * _nd, pipeline_mode=pl.Buffered(1))

    tok_spec = pl.BlockSpec((None, TS, D_MODEL), lambda b, t: (b, t, 0))
    in_specs = [tok_spec] + [const_spec(a) for a in operands[1:]]

    scratch = [
        pltpu.VMEM((TS, D_MODEL), BF16),
        pltpu.VMEM((TS, GLA_KEY_WIDTH), F32),
        pltpu.VMEM((TS, GLA_KEY_WIDTH), F32),
        pltpu.VMEM((TS, GLA_WIDTH), BF16),
        pltpu.VMEM((TS, GLA_WIDTH), F32),
        pltpu.VMEM((TS, GLA_KEY_WIDTH), F32),
        pltpu.VMEM((CG, GLA_KEY_WIDTH), F32),
        pltpu.VMEM((HIST + TS, MLSTM_WIDTH), F32),
        pltpu.VMEM((TS, MLSTM_WIDTH), F32),
        pltpu.VMEM((TS, MLSTM_WIDTH), F32),
        pltpu.VMEM((TS, MLSTM_WIDTH), BF16),
        pltpu.VMEM((TS, MLSTM_WIDTH), BF16),
        pltpu.VMEM((TS, MLSTM_WIDTH), BF16),
        pltpu.VMEM((TS, GLA_WIDTH + MLSTM_WIDTH), BF16),
        pltpu.VMEM((GLA_HEADS, GLA_DV, GLA_DK), F32),
        pltpu.VMEM((MLSTM_HEADS, MLSTM_DH, MLSTM_DH), F32),
        pltpu.VMEM((MLSTM_HEADS, SUBLANES, MLSTM_DH), F32),
        pltpu.VMEM((MLSTM_HEADS, SUBLANES, LANES), F32),
    ]

    return pl.pallas_call(
        _layer_kernel,
        out_shape=jax.ShapeDtypeStruct(x.shape, x.dtype),
        grid=(bsz, seq // TS),
        in_specs=in_specs,
        out_specs=pl.BlockSpec((None, TS, D_MODEL), lambda b, t: (b, t, 0)),
        scratch_shapes=scratch,
        compiler_params=pltpu.CompilerParams(
            dimension_semantics=("arbitrary", "arbitrary"),
            vmem_limit_bytes=VMEM_LIMIT_BYTES),
        name="gla_mlstm_layer",
    )(*operands)
```

```python
import math

import jax
import jax.numpy as jnp
from jax import lax
from jax.experimental import pallas as pl
from jax.experimental.pallas import tpu as pltpu

F32 = jnp.float32
BF16 = jnp.bfloat16

D_MODEL = 1024
GLA_HEADS = 4
GLA_DK = 128
GLA_DV = 256
GLA_KEY_WIDTH = GLA_HEADS * GLA_DK
GLA_WIDTH = GLA_HEADS * GLA_DV
GLA_GATE_RANK = 16
GLA_GATE_TAU = 16.0
MLSTM_HEADS = 4
MLSTM_DH = 256
MLSTM_WIDTH = MLSTM_HEADS * MLSTM_DH
QKV_BLOCK = 4
CONV_WIDTH = 4
EPS = 1e-6

LANES = 128
SUBLANES = 8
TS = 256
CG = 128
GLA_LEVELS = 7
HIST = SUBLANES
VMEM_LIMIT_BYTES = 60 * 1024 * 1024
LOG2E = math.log2(math.e)


def _dot(a, b):
    return jnp.dot(a, b, preferred_element_type=F32)


def _dot_nt(a, b):
    return lax.dot_general(a, b, (((1,), (1,)), ((), ())), preferred_element_type=F32)


def _dot_tn(a, b):
    return lax.dot_general(a, b, (((0,), (0,)), ((), ())), preferred_element_type=F32)


def _log_sigmoid(z):
    return jnp.minimum(z, 0.0) - jnp.log(1.0 + jnp.exp(-jnp.abs(z)))


def _silu(z):
    half = 0.5 * z
    return half * jnp.tanh(half) + half


def _split3(x):
    h1 = x.astype(BF16)
    r1 = x - h1.astype(F32)
    h2 = r1.astype(BF16)
    h3 = (r1 - h2.astype(F32)).astype(BF16)
    return h1, h2, h3


def _cumsum_rows(tri, x):
    h1, h2, h3 = _split3(x)
    return _dot(tri, h1) + _dot(tri, h2) + _dot(tri, h3)


def _cumsum_rows2(tri, x):
    h1 = x.astype(BF16)
    h2 = (x - h1.astype(F32)).astype(BF16)
    return _dot(tri, h1) + _dot(tri, h2)


def _tri(n):
    row = jnp.arange(n)[:, None]
    col = jnp.arange(n)[None, :]
    return jnp.where(row >= col, 1.0, 0.0).astype(BF16)


def _pair_level(n, levels):
    row = jnp.arange(n, dtype=jnp.int32)[:, None]
    col = jnp.arange(n, dtype=jnp.int32)[None, :]
    x = jnp.bitwise_xor(row, col)
    lvl = jnp.full((n, n), -1, jnp.int32)
    for p in range(levels):
        lvl = jnp.where(jnp.right_shift(x, p) == 1, p, lvl)
    return jnp.where(row > col, lvl, -1)


def _level_exp2_scale(n, levels):
    row = jnp.arange(n, dtype=jnp.int32)[None, :, None]
    p = jnp.arange(levels, dtype=jnp.int32)[:, None, None]
    upper = jnp.bitwise_and(jnp.right_shift(row, p), 1) == 1
    return jnp.broadcast_to(jnp.where(upper, LOG2E, -LOG2E), (levels, n, LANES)).astype(F32)


def _midpoint_rows(b_ref, lanes, p):
    s = 1 << (p + 1)
    half = s // 2
    if s >= SUBLANES:
        pieces = []
        for g in range(CG // s):
            row = g * s + half - 1
            pieces.append(jnp.broadcast_to(b_ref[row:row + 1, lanes], (s, LANES)))
        return pieces[0] if len(pieces) == 1 else jnp.concatenate(pieces, axis=0)
    assert s == 4
    sub = lax.broadcasted_iota(jnp.int32, (SUBLANES, LANES), 0)
    pieces = []
    for g in range(CG // SUBLANES):
        lo = jnp.broadcast_to(b_ref[g * 8 + 1:g * 8 + 2, lanes], (SUBLANES, LANES))
        hi = jnp.broadcast_to(b_ref[g * 8 + 5:g * 8 + 6, lanes], (SUBLANES, LANES))
        pieces.append(jnp.where(sub < 4, lo, hi))
    return jnp.concatenate(pieces, axis=0)


def _layer_kernel(x_ref, ng_ref, wmain_ref, wr_ref, wup_ref, bg_ref, glag_ref, cw_ref, cb_ref,
                  bdqk_ref, bdv_ref, wgate_ref, bgate_ref, mng_ref, skip_ref, wout_ref, fng_ref,
                  trig_ref, lvl_ref, escale_ref, trim_ref,
                  out_ref,
                  u_s, qg_s, kg_s, vg_s, zg_s, la_s, b_s, xm_s, zm_s, cact_s, qm_s, km_s, vm_s,
                  mix_s, sg_s, cm_s, nm_s, mm_s):
    t = pl.program_id(1)

    @pl.when(t == 0)
    def _reset_state():
        sg_s[...] = jnp.zeros_like(sg_s)
        cm_s[...] = jnp.zeros_like(cm_s)
        nm_s[...] = jnp.zeros_like(nm_s)
        mm_s[...] = jnp.zeros_like(mm_s)
        xm_s[0:HIST, :] = jnp.zeros((HIST, MLSTM_WIDTH), F32)

    x = x_ref[...]
    ms = jnp.mean(x * x, axis=-1, keepdims=True)
    u_s[...] = (x * lax.rsqrt(ms + EPS) * ng_ref[...]).astype(BF16)
    u = u_s[...]
    r_g = _dot(u, wr_ref[...]).astype(BF16)
    gate = _dot(r_g, wup_ref[...]) + bg_ref[...]
    la_s[...] = _log_sigmoid(gate) * (1.0 / GLA_GATE_TAU)
    c0 = 0
    qg_s[...] = _dot(u, wmain_ref[:, c0:c0 + GLA_KEY_WIDTH]) * (GLA_DK ** -0.5)
    c0 += GLA_KEY_WIDTH
    kg_s[...] = _dot(u, wmain_ref[:, c0:c0 + GLA_KEY_WIDTH])
    c0 += GLA_KEY_WIDTH
    vg_s[...] = _dot(u, wmain_ref[:, c0:c0 + GLA_WIDTH]).astype(BF16)
    c0 += GLA_WIDTH
    zg_s[...] = _dot(u, wmain_ref[:, c0:c0 + GLA_WIDTH])
    c0 += GLA_WIDTH
    xm_s[HIST:HIST + TS, :] = _dot(u, wmain_ref[:, c0:c0 + MLSTM_WIDTH])
    c0 += MLSTM_WIDTH
    zm_s[...] = _dot(u, wmain_ref[:, c0:c0 + MLSTM_WIDTH])

    lvl = lvl_ref[...]
    for c in range(TS // CG):
        rows = slice(c * CG, (c + 1) * CG)
        b_s[...] = _cumsum_rows2(trig_ref[...], la_s[rows, :])
        for h in range(GLA_HEADS):
            kl = slice(h * GLA_DK, (h + 1) * GLA_DK)
            vl = slice(h * GLA_DV, (h + 1) * GLA_DV)
            q = qg_s[rows, kl]
            k = kg_s[rows, kl]
            b = b_s[:, kl]
            q16 = q.astype(BF16)
            k16 = k.astype(BF16)
            e0 = jnp.exp(la_s[rows, kl]).astype(BF16)
            sc = jnp.where(lvl == 0, _dot_nt(q16 * e0, k16), 0.0)
            for p in range(1, GLA_LEVELS):
                e = jnp.exp2((b - _midpoint_rows(b_s, kl, p)) * escale_ref[p]).astype(BF16)
                sc = jnp.where(lvl == p, _dot_nt(q16 * e, k16 * e), sc)
            v = vg_s[rows, vl]
            st = sg_s[h]
            diag = jnp.sum(q * k, axis=-1, keepdims=True)
            o = (_dot(sc.astype(BF16), v) + diag * v.astype(F32)
                 + _dot_nt((q * jnp.exp(b)).astype(BF16), st.astype(BF16)))
            b_last = b[CG - 1:CG, :]
            k_dec = (k * jnp.exp(b_last - b)).astype(BF16)
            sg_s[h] = st * jnp.exp(b_last) + _dot_tn(v, k_dec)
            o = o * lax.rsqrt(jnp.mean(o * o, axis=-1, keepdims=True) + EPS) * glag_ref[:, vl]
            mix_s[rows, vl] = (o * _silu(zg_s[rows, vl])).astype(BF16)

    conv = cb_ref[...] + cw_ref[3:4, :] * xm_s[HIST:HIST + TS, :]
    for w in range(CONV_WIDTH - 1):
        off = HIST - (CONV_WIDTH - 1) + w
        conv = conv + cw_ref[w:w + 1, :] * xm_s[off:off + TS, :]
    cact_s[...] = _silu(conv)
    for j in range(MLSTM_WIDTH // 256):
        cl = slice(j * 256, (j + 1) * 256)
        qk = _dot(cact_s[:, cl].astype(BF16), bdqk_ref[j])
        qm_s[:, cl] = qk[:, 0:256].astype(BF16)
        km_s[:, cl] = qk[:, 256:512].astype(BF16)
        vm_s[:, cl] = _dot(xm_s[HIST:HIST + TS, cl].astype(BF16), bdv_ref[j]).astype(BF16)
    xm_s[0:HIST, :] = xm_s[TS:TS + HIST, :]
    g = (_dot(qm_s[...], wgate_ref[0:MLSTM_WIDTH, :])
         + _dot(km_s[...], wgate_ref[MLSTM_WIDTH:2 * MLSTM_WIDTH, :])
         + _dot(vm_s[...], wgate_ref[2 * MLSTM_WIDTH:3 * MLSTM_WIDTH, :])
         + bgate_ref[...])
    tri_m = trim_ref[...]
    causal = (lax.broadcasted_iota(jnp.int32, (TS, TS), 0)
              >= lax.broadcasted_iota(jnp.int32, (TS, TS), 1))
    bcum = _cumsum_rows(tri_m, _log_sigmoid(g))
    bcum = pltpu.roll(bcum, LANES - MLSTM_HEADS, axis=1)
    a_all = g - bcum
    a_all_t = a_all.T
    log_kscale = 0.5 * math.log(MLSTM_DH)

    for h in range(MLSTM_HEADS):
        hl = slice(h * MLSTM_DH, (h + 1) * MLSTM_DH)
        a_row = a_all_t[h:h + 1, :]
        a_col = a_all[:, h:h + 1]
        b_col = bcum[:, h:h + 1]
        m_prev = mm_s[h][0:1, 0:1]
        run_max = jnp.max(jnp.where(causal, a_row, -jnp.inf), axis=-1, keepdims=True)
        m_rel = jnp.maximum(m_prev, run_max)
        dmat = jnp.exp(jnp.where(causal, (a_row - log_kscale) - m_rel, -jnp.inf))
        q = qm_s[:, hl]
        k = km_s[:, hl]
        v = vm_s[:, hl]
        s = _dot_nt(q, k) * dmat
        w_inter = jnp.exp(m_prev - m_rel)
        ct = cm_s[h]
        n_row = nm_s[h][0:1, :]
        num = _dot(s.astype(BF16), v) + w_inter * _dot(q, ct.astype(BF16))
        den = (jnp.sum(s, axis=-1, keepdims=True)
               + w_inter * jnp.sum(q.astype(F32) * n_row, axis=-1, keepdims=True))
        m_abs = b_col + m_rel
        hh = num * (1.0 / jnp.maximum(jnp.abs(den), jnp.exp(-m_abs)))
        hh = hh * lax.rsqrt(jnp.mean(hh * hh, axis=-1, keepdims=True) + EPS) * mng_ref[:, hl]
        o = (hh + skip_ref[:, hl] * cact_s[:, hl]) * _silu(zm_s[:, hl])
        mix_s[:, GLA_WIDTH + h * MLSTM_DH:GLA_WIDTH + (h + 1) * MLSTM_DH] = o.astype(BF16)
        m_last = m_rel[TS - 1:TS, :]
        decay = jnp.exp(m_prev - m_last)
        wk_col = jnp.exp((a_col - log_kscale) - m_last)
        wk_row = jnp.exp((a_row - log_kscale) - m_last)
        vw = (v.astype(F32) * wk_col).astype(BF16)
        cm_s[h] = decay * ct + _dot_tn(k, vw)
        n_add = _dot(jnp.broadcast_to(wk_row, (SUBLANES, TS)).astype(BF16), k)
        nm_s[h] = decay * nm_s[h] + n_add
        mm_s[h] = jnp.broadcast_to(b_col[TS - 1:TS, :] + m_last, (SUBLANES, LANES))

    y = x_ref[...] + _dot(mix_s[...], wout_ref[...])
    out_ref[...] = y * lax.rsqrt(jnp.mean(y * y, axis=-1, keepdims=True) + EPS) * fng_ref[...]


def _block_diag_tiles(w):
    wf = w.reshape(-1, QKV_BLOCK).astype(BF16)
    sel = (jnp.arange(256)[None, :] % QKV_BLOCK == jnp.arange(QKV_BLOCK)[:, None]).astype(BF16)
    spread = jnp.dot(wf, sel, preferred_element_type=F32)
    r = (jnp.arange(wf.shape[0])[:, None] % 256) // QKV_BLOCK
    c = jnp.arange(256)[None, :] // QKV_BLOCK
    return jnp.where(r == c, spread, 0.0).astype(BF16).reshape(-1, 256, 256)


def kernel(x, norm_g, w_in, w_gla_gate_up, b_gla_gate, gla_norm_g, conv_w, conv_b, w_q_m, w_k_m, w_v_m, w_igate, b_igate, w_fgate, b_fgate, mlstm_norm_g, mlstm_skip, w_out, final_norm_g):
    bsz, seq, d = x.shape
    assert d == D_MODEL and seq % TS == 0

    r0 = 2 * GLA_KEY_WIDTH + 2 * GLA_WIDTH
    w_in16 = w_in.astype(BF16)
    w_main = jnp.concatenate([w_in16[:, :r0], w_in16[:, r0 + GLA_GATE_RANK:]], axis=1)
    w_r = jnp.pad(w_in16[:, r0:r0 + GLA_GATE_RANK], ((0, 0), (0, LANES - GLA_GATE_RANK)))
    w_up = jnp.pad(w_gla_gate_up, ((0, LANES - GLA_GATE_RANK), (0, 0))).astype(BF16)
    bd_qk = jnp.concatenate([_block_diag_tiles(w_q_m), _block_diag_tiles(w_k_m)], axis=2)
    bd_v = _block_diag_tiles(w_v_m)
    w_gate = jnp.pad(jnp.concatenate([w_igate, w_fgate], axis=1),
                     ((0, 0), (0, LANES - 2 * MLSTM_HEADS))).astype(BF16)
    b_gate = jnp.pad(jnp.concatenate([b_igate, b_fgate]), (0, LANES - 2 * MLSTM_HEADS)).reshape(1, LANES)
    conv_w8 = jnp.pad(conv_w, ((0, SUBLANES - CONV_WIDTH), (0, 0)))
    row = lambda a: a.reshape(1, -1)

    operands = [
        x, row(norm_g), w_main, w_r, w_up, row(b_gla_gate), row(gla_norm_g), conv_w8, row(conv_b),
        bd_qk, bd_v, w_gate, b_gate, row(mlstm_norm_g), row(mlstm_skip), w_out.astype(BF16),
        row(final_norm_g),
        _tri(CG), _pair_level(CG, GLA_LEVELS), _level_exp2_scale(CG, GLA_LEVELS), _tri(TS),
    ]

    def const_spec(a):
        nd = a.ndim
        return pl.BlockSpec(a.shape, lambda b, t, _nd=nd: (0,) * _nd, pipeline_mode=pl.Buffered(1))

    tok_spec = pl.BlockSpec((None, TS, D_MODEL), lambda b, t: (b, t, 0))
    in_specs = [tok_spec] + [const_spec(a) for a in operands[1:]]

    scratch = [
        pltpu.VMEM((TS, D_MODEL), BF16),
        pltpu.VMEM((TS, GLA_KEY_WIDTH), F32),
        pltpu.VMEM((TS, GLA_KEY_WIDTH), F32),
        pltpu.VMEM((TS, GLA_WIDTH), BF16),
        pltpu.VMEM((TS, GLA_WIDTH), F32),
        pltpu.VMEM((TS, GLA_KEY_WIDTH), F32),
        pltpu.VMEM((CG, GLA_KEY_WIDTH), F32),
        pltpu.VMEM((HIST + TS, MLSTM_WIDTH), F32),
        pltpu.VMEM((TS, MLSTM_WIDTH), F32),
        pltpu.VMEM((TS, MLSTM_WIDTH), F32),
        pltpu.VMEM((TS, MLSTM_WIDTH), BF16),
        pltpu.VMEM((TS, MLSTM_WIDTH), BF16),
        pltpu.VMEM((TS, MLSTM_WIDTH), BF16),
        pltpu.VMEM((TS, GLA_WIDTH + MLSTM_WIDTH), BF16),
        pltpu.VMEM((GLA_HEADS, GLA_DV, GLA_DK), F32),
        pltpu.VMEM((MLSTM_HEADS, MLSTM_DH, MLSTM_DH), F32),
        pltpu.VMEM((MLSTM_HEADS, SUBLANES, MLSTM_DH), F32),
        pltpu.VMEM((MLSTM_HEADS, SUBLANES, LANES), F32),
    ]

    return pl.pallas_call(
        _layer_kernel,
        out_shape=jax.ShapeDtypeStruct(x.shape, x.dtype),
        grid=(bsz, seq // TS),
        in_specs=in_specs,
        out_specs=pl.BlockSpec((None, TS, D_MODEL), lambda b, t: (b, t, 0)),
        scratch_shapes=scratch,
        compiler_params=pltpu.CompilerParams(
            dimension_semantics=("arbitrary", "arbitrary"),
            vmem_limit_bytes=VMEM_LIMIT_BYTES),
        name="gla_mlstm_layer",
    )(*operands)
```

```python
import math

import jax
import jax.numpy as jnp
from jax import lax
from jax.experimental import pallas as pl
from jax.experimental.pallas import tpu as pltpu

F32 = jnp.float32
BF16 = jnp.bfloat16

D_MODEL = 1024
GLA_HEADS = 4
GLA_DK = 128
GLA_DV = 256
GLA_KEY_WIDTH = GLA_HEADS * GLA_DK
GLA_WIDTH = GLA_HEADS * GLA_DV
GLA_GATE_RANK = 16
GLA_GATE_TAU = 16.0
MLSTM_HEADS = 4
MLSTM_DH = 256
MLSTM_WIDTH = MLSTM_HEADS * MLSTM_DH
QKV_BLOCK = 4
CONV_WIDTH = 4
EPS = 1e-6

LANES = 128
SUBLANES = 8
TS = 256
CG = 128
GLA_LEVELS = 7
HIST = SUBLANES
VMEM_LIMIT_BYTES = 60 * 1024 * 1024
LOG2E = math.log2(math.e)


def _dot(a, b):
    return jnp.dot(a, b, preferred_element_type=F32)


def _dot_nt(a, b):
    return lax.dot_general(a, b, (((1,), (1,)), ((), ())), preferred_element_type=F32)


def _dot_tn(a, b):
    return lax.dot_general(a, b, (((0,), (0,)), ((), ())), preferred_element_type=F32)


def _log_sigmoid(z):
    return jnp.minimum(z, 0.0) - jnp.log(1.0 + jnp.exp(-jnp.abs(z)))


def _silu(z):
    half = 0.5 * z
    return half * jnp.tanh(half) + half


def _split3(x):
    h1 = x.astype(BF16)
    r1 = x - h1.astype(F32)
    h2 = r1.astype(BF16)
    h3 = (r1 - h2.astype(F32)).astype(BF16)
    return h1, h2, h3


def _cumsum_rows(tri, x):
    h1, h2, h3 = _split3(x)
    return _dot(tri, h1) + _dot(tri, h2) + _dot(tri, h3)


def _cumsum_rows2(tri, x):
    h1 = x.astype(BF16)
    h2 = (x - h1.astype(F32)).astype(BF16)
    return _dot(tri, h1) + _dot(tri, h2)


def _tri(n):
    row = jnp.arange(n)[:, None]
    col = jnp.arange(n)[None, :]
    return jnp.where(row >= col, 1.0, 0.0).astype(BF16)


def _pair_level(n, levels):
    row = jnp.arange(n, dtype=jnp.int32)[:, None]
    col = jnp.arange(n, dtype=jnp.int32)[None, :]
    x = jnp.bitwise_xor(row, col)
    lvl = jnp.full((n, n), -1, jnp.int32)
    for p in range(levels):
        lvl = jnp.where(jnp.right_shift(x, p) == 1, p, lvl)
    return jnp.where(row > col, lvl, -1)


def _level_exp2_scale(n, levels):
    row = jnp.arange(n, dtype=jnp.int32)[None, :, None]
    p = jnp.arange(levels, dtype=jnp.int32)[:, None, None]
    upper = jnp.bitwise_and(jnp.right_shift(row, p), 1) == 1
    return jnp.broadcast_to(jnp.where(upper, LOG2E, -LOG2E), (levels, n, LANES)).astype(F32)


def _midpoint_rows(b_ref, lanes, p):
    s = 1 << (p + 1)
    half = s // 2
    if s >= SUBLANES:
        pieces = []
        for g in range(CG // s):
            row = g * s + half - 1
            pieces.append(jnp.broadcast_to(b_ref[row:row + 1, lanes], (s, LANES)))
        return pieces[0] if len(pieces) == 1 else jnp.concatenate(pieces, axis=0)
    assert s == 4
    sub = lax.broadcasted_iota(jnp.int32, (SUBLANES, LANES), 0)
    pieces = []
    for g in range(CG // SUBLANES):
        lo = jnp.broadcast_to(b_ref[g * 8 + 1:g * 8 + 2, lanes], (SUBLANES, LANES))
        hi = jnp.broadcast_to(b_ref[g * 8 + 5:g * 8 + 6, lanes], (SUBLANES, LANES))
        pieces.append(jnp.where(sub < 4, lo, hi))
    return jnp.concatenate(pieces, axis=0)


def _layer_kernel(x_ref, ng_ref, wmain_ref, wr_ref, wup_ref, bg_ref, glag_ref, cw_ref, cb_ref,
                  bdqk_ref, bdv_ref, wgate_ref, bgate_ref, mng_ref, skip_ref, wout_ref, fng_ref,
                  trig_ref, lvl_ref, escale_ref, trim_ref,
                  out_ref,
                  u_s, qg_s, kg_s, vg_s, zg_s, la_s, b_s, xm_s, zm_s, cact_s, qm_s, km_s, vm_s,
                  mix_s, sg_s, cm_s, nm_s, mm_s):
    t = pl.program_id(1)

    @pl.when(t == 0)
    def _reset_state():
        sg_s[...] = jnp.zeros_like(sg_s)
        cm_s[...] = jnp.zeros_like(cm_s)
        nm_s[...] = jnp.zeros_like(nm_s)
        mm_s[...] = jnp.zeros_like(mm_s)
        xm_s[0:HIST, :] = jnp.zeros((HIST, MLSTM_WIDTH), F32)

    x = x_ref[...]
    ms = jnp.mean(x * x, axis=-1, keepdims=True)
    u_s[...] = (x * lax.rsqrt(ms + EPS) * ng_ref[...]).astype(BF16)
    u = u_s[...]
    r_g = _dot(u, wr_ref[...]).astype(BF16)
    gate = _dot(r_g, wup_ref[...]) + bg_ref[...]
    la_s[...] = _log_sigmoid(gate) * (1.0 / GLA_GATE_TAU)
    c0 = 0
    qg_s[...] = _dot(u, wmain_ref[:, c0:c0 + GLA_KEY_WIDTH]) * (GLA_DK ** -0.5)
    c0 += GLA_KEY_WIDTH
    kg_s[...] = _dot(u, wmain_ref[:, c0:c0 + GLA_KEY_WIDTH])
    c0 += GLA_KEY_WIDTH
    vg_s[...] = _dot(u, wmain_ref[:, c0:c0 + GLA_WIDTH]).astype(BF16)
    c0 += GLA_WIDTH
    zg_s[...] = _dot(u, wmain_ref[:, c0:c0 + GLA_WIDTH])
    c0 += GLA_WIDTH
    xm_s[HIST:HIST + TS, :] = _dot(u, wmain_ref[:, c0:c0 + MLSTM_WIDTH])
    c0 += MLSTM_WIDTH
    zm_s[...] = _dot(u, wmain_ref[:, c0:c0 + MLSTM_WIDTH])

    front = {}

    def front_conv():
        conv = cb_ref[...] + cw_ref[3:4, :] * xm_s[HIST:HIST + TS, :]
        for w in range(CONV_WIDTH - 1):
            off = HIST - (CONV_WIDTH - 1) + w
            conv = conv + cw_ref[w:w + 1, :] * xm_s[off:off + TS, :]
        cact_s[...] = _silu(conv)

    def front_headwise(j):
        cl = slice(j * 256, (j + 1) * 256)
        qk = _dot(cact_s[:, cl].astype(BF16), bdqk_ref[j])
        qm_s[:, cl] = qk[:, 0:256].astype(BF16)
        km_s[:, cl] = qk[:, 256:512].astype(BF16)
        vm_s[:, cl] = _dot(xm_s[HIST:HIST + TS, cl].astype(BF16), bdv_ref[j]).astype(BF16)

    def front_gates():
        xm_s[0:HIST, :] = xm_s[TS:TS + HIST, :]
        front["g"] = (_dot(qm_s[...], wgate_ref[0:MLSTM_WIDTH, :])
                      + _dot(km_s[...], wgate_ref[MLSTM_WIDTH:2 * MLSTM_WIDTH, :])
                      + _dot(vm_s[...], wgate_ref[2 * MLSTM_WIDTH:3 * MLSTM_WIDTH, :])
                      + bgate_ref[...])

    def front_decay():
        g = front["g"]
        bcum = _cumsum_rows(trim_ref[...], _log_sigmoid(g))
        bcum = pltpu.roll(bcum, LANES - MLSTM_HEADS, axis=1)
        front["bcum"] = bcum
        front["a_all"] = g - bcum
        front["a_all_t"] = front["a_all"].T

    front_pieces = [front_conv,
                    lambda: (front_headwise(0), front_headwise(1)),
                    lambda: (front_headwise(2), front_headwise(3)),
                    front_gates, front_decay]

    lvl = lvl_ref[...]
    for c in range(TS // CG):
        rows = slice(c * CG, (c + 1) * CG)
        b_s[...] = _cumsum_rows2(trig_ref[...], la_s[rows, :])
        for h in range(GLA_HEADS):
            kl = slice(h * GLA_DK, (h + 1) * GLA_DK)
            vl = slice(h * GLA_DV, (h + 1) * GLA_DV)
            q = qg_s[rows, kl]
            k = kg_s[rows, kl]
            b = b_s[:, kl]
            q16 = q.astype(BF16)
            k16 = k.astype(BF16)
            e0 = jnp.exp(la_s[rows, kl]).astype(BF16)
            sc = jnp.where(lvl == 0, _dot_nt(q16 * e0, k16), 0.0)
            for p in range(1, GLA_LEVELS):
                e = jnp.exp2((b - _midpoint_rows(b_s, kl, p)) * escale_ref[p]).astype(BF16)
                sc = jnp.where(lvl == p, _dot_nt(q16 * e, k16 * e), sc)
            v = vg_s[rows, vl]
            st = sg_s[h]
            diag = jnp.sum(q * k, axis=-1, keepdims=True)
            o = (_dot(sc.astype(BF16), v) + diag * v.astype(F32)
                 + _dot_nt((q * jnp.exp(b)).astype(BF16), st.astype(BF16)))
            b_last = b[CG - 1:CG, :]
            k_dec = (k * jnp.exp(b_last - b)).astype(BF16)
            sg_s[h] = st * jnp.exp(b_last) + _dot_tn(v, k_dec)
            o = o * lax.rsqrt(jnp.mean(o * o, axis=-1, keepdims=True) + EPS) * glag_ref[:, vl]
            mix_s[rows, vl] = (o * _silu(zg_s[rows, vl])).astype(BF16)
            if front_pieces:
                front_pieces.pop(0)()

    assert not front_pieces
    bcum, a_all, a_all_t = front["bcum"], front["a_all"], front["a_all_t"]
    causal = (lax.broadcasted_iota(jnp.int32, (TS, TS), 0)
              >= lax.broadcasted_iota(jnp.int32, (TS, TS), 1))
    log_kscale = 0.5 * math.log(MLSTM_DH)
    for h in range(MLSTM_HEADS):
        hl = slice(h * MLSTM_DH, (h + 1) * MLSTM_DH)
        a_row = a_all_t[h:h + 1, :]
        a_col = a_all[:, h:h + 1]
        b_col = bcum[:, h:h + 1]
        m_prev = mm_s[h][0:1, 0:1]
        run_max = jnp.max(jnp.where(causal, a_row, -jnp.inf), axis=-1, keepdims=True)
        m_rel = jnp.maximum(m_prev, run_max)
        dmat = jnp.exp(jnp.where(causal, (a_row - log_kscale) - m_rel, -jnp.inf))
        q = qm_s[:, hl]
        k = km_s[:, hl]
        v = vm_s[:, hl]
        s = _dot_nt(q, k) * dmat
        w_inter = jnp.exp(m_prev - m_rel)
        ct = cm_s[h]
        n_row = nm_s[h][0:1, :]
        num = _dot(s.astype(BF16), v) + w_inter * _dot(q, ct.astype(BF16))
        den = (jnp.sum(s, axis=-1, keepdims=True)
               + w_inter * jnp.sum(q.astype(F32) * n_row, axis=-1, keepdims=True))
        m_abs = b_col + m_rel
        hh = num * (1.0 / jnp.maximum(jnp.abs(den), jnp.exp(-m_abs)))
        hh = hh * lax.rsqrt(jnp.mean(hh * hh, axis=-1, keepdims=True) + EPS) * mng_ref[:, hl]
        o = (hh + skip_ref[:, hl] * cact_s[:, hl]) * _silu(zm_s[:, hl])
        mix_s[:, GLA_WIDTH + h * MLSTM_DH:GLA_WIDTH + (h + 1) * MLSTM_DH] = o.astype(BF16)
        m_last = m_rel[TS - 1:TS, :]
        decay = jnp.exp(m_prev - m_last)
        wk_col = jnp.exp((a_col - log_kscale) - m_last)
        wk_row = jnp.exp((a_row - log_kscale) - m_last)
        vw = (v.astype(F32) * wk_col).astype(BF16)
        cm_s[h] = decay * ct + _dot_tn(k, vw)
        n_add = _dot(jnp.broadcast_to(wk_row, (SUBLANES, TS)).astype(BF16), k)
        nm_s[h] = decay * nm_s[h] + n_add
        mm_s[h] = jnp.broadcast_to(b_col[TS - 1:TS, :] + m_last, (SUBLANES, LANES))

    y = x_ref[...] + _dot(mix_s[...], wout_ref[...])
    out_ref[...] = y * lax.rsqrt(jnp.mean(y * y, axis=-1, keepdims=True) + EPS) * fng_ref[...]


def _block_diag_tiles(w):
    wf = w.reshape(-1, QKV_BLOCK).astype(BF16)
    sel = (jnp.arange(256)[None, :] % QKV_BLOCK == jnp.arange(QKV_BLOCK)[:, None]).astype(BF16)
    spread = jnp.dot(wf, sel, preferred_element_type=F32)
    r = (jnp.arange(wf.shape[0])[:, None] % 256) // QKV_BLOCK
    c = jnp.arange(256)[None, :] // QKV_BLOCK
    return jnp.where(r == c, spread, 0.0).astype(BF16).reshape(-1, 256, 256)


def kernel(x, norm_g, w_in, w_gla_gate_up, b_gla_gate, gla_norm_g, conv_w, conv_b, w_q_m, w_k_m, w_v_m, w_igate, b_igate, w_fgate, b_fgate, mlstm_norm_g, mlstm_skip, w_out, final_norm_g):
    bsz, seq, d = x.shape
    assert d == D_MODEL and seq % TS == 0

    r0 = 2 * GLA_KEY_WIDTH + 2 * GLA_WIDTH
    w_in16 = w_in.astype(BF16)
    w_main = jnp.concatenate([w_in16[:, :r0], w_in16[:, r0 + GLA_GATE_RANK:]], axis=1)
    w_r = jnp.pad(w_in16[:, r0:r0 + GLA_GATE_RANK], ((0, 0), (0, LANES - GLA_GATE_RANK)))
    w_up = jnp.pad(w_gla_gate_up, ((0, LANES - GLA_GATE_RANK), (0, 0))).astype(BF16)
    bd_qk = jnp.concatenate([_block_diag_tiles(w_q_m), _block_diag_tiles(w_k_m)], axis=2)
    bd_v = _block_diag_tiles(w_v_m)
    w_gate = jnp.pad(jnp.concatenate([w_igate, w_fgate], axis=1),
                     ((0, 0), (0, LANES - 2 * MLSTM_HEADS))).astype(BF16)
    b_gate = jnp.pad(jnp.concatenate([b_igate, b_fgate]), (0, LANES - 2 * MLSTM_HEADS)).reshape(1, LANES)
    conv_w8 = jnp.pad(conv_w, ((0, SUBLANES - CONV_WIDTH), (0, 0)))
    row = lambda a: a.reshape(1, -1)

    operands = [
        x, row(norm_g), w_main, w_r, w_up, row(b_gla_gate), row(gla_norm_g), conv_w8, row(conv_b),
        bd_qk, bd_v, w_gate, b_gate, row(mlstm_norm_g), row(mlstm_skip), w_out.astype(BF16),
        row(final_norm_g),
        _tri(CG), _pair_level(CG, GLA_LEVELS), _level_exp2_scale(CG, GLA_LEVELS), _tri(TS),
    ]

    def const_spec(a):
        nd = a.ndim
        return pl.BlockSpec(a.shape, lambda b, t, _nd=nd: (0,) * _nd, pipeline_mode=pl.Buffered(1))

    tok_spec = pl.BlockSpec((None, TS, D_MODEL), lambda b, t: (b, t, 0))
    in_specs = [tok_spec] + [const_spec(a) for a in operands[1:]]

    scratch = [
        pltpu.VMEM((TS, D_MODEL), BF16),
        pltpu.VMEM((TS, GLA_KEY_WIDTH), F32),
        pltpu.VMEM((TS, GLA_KEY_WIDTH), F32),
        pltpu.VMEM((TS, GLA_WIDTH), BF16),
        pltpu.VMEM((TS, GLA_WIDTH), F32),
        pltpu.VMEM((TS, GLA_KEY_WIDTH), F32),
        pltpu.VMEM((CG, GLA_KEY_WIDTH), F32),
        pltpu.VMEM((HIST + TS, MLSTM_WIDTH), F32),
        pltpu.VMEM((TS, MLSTM_WIDTH), F32),
        pltpu.VMEM((TS, MLSTM_WIDTH), F32),
        pltpu.VMEM((TS, MLSTM_WIDTH), BF16),
        pltpu.VMEM((TS, MLSTM_WIDTH), BF16),
        pltpu.VMEM((TS, MLSTM_WIDTH), BF16),
        pltpu.VMEM((TS, GLA_WIDTH + MLSTM_WIDTH), BF16),
        pltpu.VMEM((GLA_HEADS, GLA_DV, GLA_DK), F32),
        pltpu.VMEM((MLSTM_HEADS, MLSTM_DH, MLSTM_DH), F32),
        pltpu.VMEM((MLSTM_HEADS, SUBLANES, MLSTM_DH), F32),
        pltpu.VMEM((MLSTM_HEADS, SUBLANES, LANES), F32),
    ]

    return pl.pallas_call(
        _layer_kernel,
        out_shape=jax.ShapeDtypeStruct(x.shape, x.dtype),
        grid=(bsz, seq // TS),
        in_specs=in_specs,
        out_specs=pl.BlockSpec((None, TS, D_MODEL), lambda b, t: (b, t, 0)),
        scratch_shapes=scratch,
        compiler_params=pltpu.CompilerParams(
            dimension_semantics=("arbitrary", "arbitrary"),
            vmem_limit_bytes=VMEM_LIMIT_BYTES),
        name="gla_mlstm_layer",
    )(*operands)
```

```python
import math

import jax
import jax.numpy as jnp
from jax import lax
from jax.experimental import pallas as pl
from jax.experimental.pallas import tpu as pltpu

F32 = jnp.float32
BF16 = jnp.bfloat16

D_MODEL = 1024
GLA_HEADS = 4
GLA_DK = 128
GLA_DV = 256
GLA_KEY_WIDTH = GLA_HEADS * GLA_DK
GLA_WIDTH = GLA_HEADS * GLA_DV
GLA_GATE_RANK = 16
GLA_GATE_TAU = 16.0
MLSTM_HEADS = 4
MLSTM_DH = 256
MLSTM_WIDTH = MLSTM_HEADS * MLSTM_DH
QKV_BLOCK = 4
CONV_WIDTH = 4
EPS = 1e-6

LANES = 128
SUBLANES = 8
TS = 256
TB = 1024
CG = 128
GLA_LEVELS = 7
HIST = SUBLANES
VMEM_LIMIT_BYTES = 60 * 1024 * 1024
LOG2E = math.log2(math.e)


def _dot(a, b):
    return jnp.dot(a, b, preferred_element_type=F32)


def _dot_nt(a, b):
    return lax.dot_general(a, b, (((1,), (1,)), ((), ())), preferred_element_type=F32)


def _dot_tn(a, b):
    return lax.dot_general(a, b, (((0,), (0,)), ((), ())), preferred_element_type=F32)


def _log_sigmoid(z):
    return jnp.minimum(z, 0.0) - jnp.log(1.0 + jnp.exp(-jnp.abs(z)))


def _silu(z):
    half = 0.5 * z
    return half * jnp.tanh(half) + half


def _split3(x):
    h1 = x.astype(BF16)
    r1 = x - h1.astype(F32)
    h2 = r1.astype(BF16)
    h3 = (r1 - h2.astype(F32)).astype(BF16)
    return h1, h2, h3


def _cumsum_rows(tri, x):
    h1, h2, h3 = _split3(x)
    return _dot(tri, h1) + _dot(tri, h2) + _dot(tri, h3)


def _cumsum_rows2(tri, x):
    h1 = x.astype(BF16)
    h2 = (x - h1.astype(F32)).astype(BF16)
    return _dot(tri, h1) + _dot(tri, h2)


def _tri(n):
    row = jnp.arange(n)[:, None]
    col = jnp.arange(n)[None, :]
    return jnp.where(row >= col, 1.0, 0.0).astype(BF16)


def _pair_level(n, levels):
    row = jnp.arange(n, dtype=jnp.int32)[:, None]
    col = jnp.arange(n, dtype=jnp.int32)[None, :]
    x = jnp.bitwise_xor(row, col)
    lvl = jnp.full((n, n), -1, jnp.int32)
    for p in range(levels):
        lvl = jnp.where(jnp.right_shift(x, p) == 1, p, lvl)
    return jnp.where(row > col, lvl, -1)


def _level_exp2_scale(n, levels):
    row = jnp.arange(n, dtype=jnp.int32)[None, :, None]
    p = jnp.arange(levels, dtype=jnp.int32)[:, None, None]
    upper = jnp.bitwise_and(jnp.right_shift(row, p), 1) == 1
    return jnp.broadcast_to(jnp.where(upper, LOG2E, -LOG2E), (levels, n, LANES)).astype(F32)


def _midpoint_rows(b_ref, lanes, p):
    s = 1 << (p + 1)
    half = s // 2
    if s >= SUBLANES:
        pieces = []
        for g in range(CG // s):
            row = g * s + half - 1
            pieces.append(jnp.broadcast_to(b_ref[row:row + 1, lanes], (s, LANES)))
        return pieces[0] if len(pieces) == 1 else jnp.concatenate(pieces, axis=0)
    assert s == 4
    sub = lax.broadcasted_iota(jnp.int32, (SUBLANES, LANES), 0)
    pieces = []
    for g in range(CG // SUBLANES):
        lo = jnp.broadcast_to(b_ref[g * 8 + 1:g * 8 + 2, lanes], (SUBLANES, LANES))
        hi = jnp.broadcast_to(b_ref[g * 8 + 5:g * 8 + 6, lanes], (SUBLANES, LANES))
        pieces.append(jnp.where(sub < 4, lo, hi))
    return jnp.concatenate(pieces, axis=0)


def _layer_kernel(x_ref, ng_ref, wgla_ref, wml_ref, wr_ref, wup_ref, bg_ref, glag_ref, cw_ref, cb_ref,
                  bdqk_ref, bdv_ref, wgate_ref, bgate_ref, mng_ref, skip_ref, wout_ref, fng_ref,
                  trig_ref, lvl_ref, escale_ref, trim_ref,
                  out_ref,
                  u_s, qg_s, kg_s, vg_s, zg_s, la_s, b_s, xm_s, zm_s, cact_s, qm_s, km_s, vm_s,
                  mix_s, sg_s, cm_s, nm_s, mm_s):
    @pl.when(pl.program_id(1) == 0)
    def _reset_state():
        sg_s[...] = jnp.zeros_like(sg_s)
        cm_s[...] = jnp.zeros_like(cm_s)
        nm_s[...] = jnp.zeros_like(nm_s)
        mm_s[...] = jnp.zeros_like(mm_s)
        xm_s[0:HIST, :] = jnp.zeros((HIST, MLSTM_WIDTH), F32)

    def tile(i, carry):
        _process_tile(pl.multiple_of(i * TS, TS),
                      x_ref, ng_ref, wgla_ref, wml_ref, wr_ref, wup_ref, bg_ref, glag_ref, cw_ref, cb_ref,
                      bdqk_ref, bdv_ref, wgate_ref, bgate_ref, mng_ref, skip_ref, wout_ref, fng_ref,
                      trig_ref, lvl_ref, escale_ref, trim_ref, out_ref,
                      u_s, qg_s, kg_s, vg_s, zg_s, la_s, b_s, xm_s, zm_s, cact_s, qm_s, km_s, vm_s,
                      mix_s, sg_s, cm_s, nm_s, mm_s)
        return carry

    lax.fori_loop(0, TB // TS, tile, 0)


def _process_tile(row0, x_ref, ng_ref, wgla_ref, wml_ref, wr_ref, wup_ref, bg_ref, glag_ref, cw_ref, cb_ref,
                  bdqk_ref, bdv_ref, wgate_ref, bgate_ref, mng_ref, skip_ref, wout_ref, fng_ref,
                  trig_ref, lvl_ref, escale_ref, trim_ref, out_ref,
                  u_s, qg_s, kg_s, vg_s, zg_s, la_s, b_s, xm_s, zm_s, cact_s, qm_s, km_s, vm_s,
                  mix_s, sg_s, cm_s, nm_s, mm_s):
    x = x_ref[pl.ds(row0, TS), :]
    ms = jnp.mean(x * x, axis=-1, keepdims=True)
    u_s[...] = (x * lax.rsqrt(ms + EPS) * ng_ref[...]).astype(BF16)
    u = u_s[...]
    r_g = _dot(u, wr_ref[...]).astype(BF16)
    gate = _dot(r_g, wup_ref[...]) + bg_ref[...]
    la_s[...] = _log_sigmoid(gate) * (1.0 / GLA_GATE_TAU)
    c0 = 0
    qg_s[...] = _dot(u, wgla_ref[:, c0:c0 + GLA_KEY_WIDTH]) * (GLA_DK ** -0.5)
    c0 += GLA_KEY_WIDTH
    kg_s[...] = _dot(u, wgla_ref[:, c0:c0 + GLA_KEY_WIDTH])
    c0 += GLA_KEY_WIDTH
    vg_s[...] = _dot(u, wgla_ref[:, c0:c0 + GLA_WIDTH]).astype(BF16)
    c0 += GLA_WIDTH
    zg_s[...] = _dot(u, wgla_ref[:, c0:c0 + GLA_WIDTH])
    xm_s[HIST:HIST + TS, :] = _dot(u, wml_ref[:, 0:MLSTM_WIDTH])
    zm_s[...] = _dot(u, wml_ref[:, MLSTM_WIDTH:2 * MLSTM_WIDTH])

    front = {}

    def front_conv():
        conv = cb_ref[...] + cw_ref[3:4, :] * xm_s[HIST:HIST + TS, :]
        for w in range(CONV_WIDTH - 1):
            off = HIST - (CONV_WIDTH - 1) + w
            conv = conv + cw_ref[w:w + 1, :] * xm_s[off:off + TS, :]
        cact_s[...] = _silu(conv)

    def front_headwise(j):
        cl = slice(j * 256, (j + 1) * 256)
        qk = _dot(cact_s[:, cl].astype(BF16), bdqk_ref[j])
        qm_s[:, cl] = qk[:, 0:256].astype(BF16)
        km_s[:, cl] = qk[:, 256:512].astype(BF16)
        vm_s[:, cl] = _dot(xm_s[HIST:HIST + TS, cl].astype(BF16), bdv_ref[j]).astype(BF16)

    def front_gates():
        xm_s[0:HIST, :] = xm_s[TS:TS + HIST, :]
        front["g"] = (_dot(qm_s[...], wgate_ref[0:MLSTM_WIDTH, :])
                      + _dot(km_s[...], wgate_ref[MLSTM_WIDTH:2 * MLSTM_WIDTH, :])
                      + _dot(vm_s[...], wgate_ref[2 * MLSTM_WIDTH:3 * MLSTM_WIDTH, :])
                      + bgate_ref[...])

    def front_decay():
        g = front["g"]
        bcum = _cumsum_rows(trim_ref[...], _log_sigmoid(g))
        bcum = pltpu.roll(bcum, LANES - MLSTM_HEADS, axis=1)
        front["bcum"] = bcum
        front["a_all"] = g - bcum
        front["a_all_t"] = front["a_all"].T

    front_pieces = [front_conv,
                    lambda: (front_headwise(0), front_headwise(1)),
                    lambda: (front_headwise(2), front_headwise(3)),
                    front_gates, front_decay]

    lvl = lvl_ref[...]
    for c in range(TS // CG):
        rows = slice(c * CG, (c + 1) * CG)
        b_s[...] = _cumsum_rows2(trig_ref[...], la_s[rows, :])
        for h in range(GLA_HEADS):
            kl = slice(h * GLA_DK, (h + 1) * GLA_DK)
            vl = slice(h * GLA_DV, (h + 1) * GLA_DV)
            q = qg_s[rows, kl]
            k = kg_s[rows, kl]
            b = b_s[:, kl]
            q16 = q.astype(BF16)
            k16 = k.astype(BF16)
            e0 = jnp.exp(la_s[rows, kl]).astype(BF16)
            sc = jnp.where(lvl == 0, _dot_nt(q16 * e0, k16), 0.0)
            for p in range(1, GLA_LEVELS):
                e = jnp.exp2((b - _midpoint_rows(b_s, kl, p)) * escale_ref[p]).astype(BF16)
                sc = jnp.where(lvl == p, _dot_nt(q16 * e, k16 * e), sc)
            v = vg_s[rows, vl]
            st = sg_s[h]
            diag = jnp.sum(q * k, axis=-1, keepdims=True)
            o = (_dot(sc.astype(BF16), v) + diag * v.astype(F32)
                 + _dot_nt((q * jnp.exp(b)).astype(BF16), st.astype(BF16)))
            b_last = b[CG - 1:CG, :]
            k_dec = (k * jnp.exp(b_last - b)).astype(BF16)
            sg_s[h] = st * jnp.exp(b_last) + _dot_tn(v, k_dec)
            o = o * lax.rsqrt(jnp.mean(o * o, axis=-1, keepdims=True) + EPS) * glag_ref[:, vl]
            mix_s[rows, vl] = (o * _silu(zg_s[rows, vl])).astype(BF16)
            if front_pieces:
                front_pieces.pop(0)()

    assert not front_pieces
    bcum, a_all, a_all_t = front["bcum"], front["a_all"], front["a_all_t"]
    causal = (lax.broadcasted_iota(jnp.int32, (TS, TS), 0)
              >= lax.broadcasted_iota(jnp.int32, (TS, TS), 1))
    log_kscale = 0.5 * math.log(MLSTM_DH)
    for h in range(MLSTM_HEADS):
        hl = slice(h * MLSTM_DH, (h + 1) * MLSTM_DH)
        a_row = a_all_t[h:h + 1, :]
        a_col = a_all[:, h:h + 1]
        b_col = bcum[:, h:h + 1]
        m_prev = mm_s[h][0:1, 0:1]
        run_max = jnp.max(jnp.where(causal, a_row, -jnp.inf), axis=-1, keepdims=True)
        m_rel = jnp.maximum(m_prev, run_max)
        dmat = jnp.exp(jnp.where(causal, (a_row - log_kscale) - m_rel, -jnp.inf))
        q = qm_s[:, hl]
        k = km_s[:, hl]
        v = vm_s[:, hl]
        s = _dot_nt(q, k) * dmat
        w_inter = jnp.exp(m_prev - m_rel)
        ct = cm_s[h]
        n_row = nm_s[h][0:1, :]
        num = _dot(s.astype(BF16), v) + w_inter * _dot(q, ct.astype(BF16))
        den = (jnp.sum(s, axis=-1, keepdims=True)
               + w_inter * jnp.sum(q.astype(F32) * n_row, axis=-1, keepdims=True))
        m_abs = b_col + m_rel
        hh = num * (1.0 / jnp.maximum(jnp.abs(den), jnp.exp(-m_abs)))
        hh = hh * lax.rsqrt(jnp.mean(hh * hh, axis=-1, keepdims=True) + EPS) * mng_ref[:, hl]
        o = (hh + skip_ref[:, hl] * cact_s[:, hl]) * _silu(zm_s[:, hl])
        mix_s[:, GLA_WIDTH + h * MLSTM_DH:GLA_WIDTH + (h + 1) * MLSTM_DH] = o.astype(BF16)
        m_last = m_rel[TS - 1:TS, :]
        decay = jnp.exp(m_prev - m_last)
        wk_col = jnp.exp((a_col - log_kscale) - m_last)
        wk_row = jnp.exp((a_row - log_kscale) - m_last)
        vw = (v.astype(F32) * wk_col).astype(BF16)
        cm_s[h] = decay * ct + _dot_tn(k, vw)
        n_add = _dot(jnp.broadcast_to(wk_row, (SUBLANES, TS)).astype(BF16), k)
        nm_s[h] = decay * nm_s[h] + n_add
        mm_s[h] = jnp.broadcast_to(b_col[TS - 1:TS, :] + m_last, (SUBLANES, LANES))

    y = x_ref[pl.ds(row0, TS), :] + _dot(mix_s[...], wout_ref[...])
    out_ref[pl.ds(row0, TS), :] = (y * lax.rsqrt(jnp.mean(y * y, axis=-1, keepdims=True) + EPS)
                                   * fng_ref[...])


def _block_diag_tiles(w):
    wf = w.reshape(-1, QKV_BLOCK).astype(BF16)
    sel = (jnp.arange(256)[None, :] % QKV_BLOCK == jnp.arange(QKV_BLOCK)[:, None]).astype(BF16)
    spread = jnp.dot(wf, sel, preferred_element_type=F32)
    r = (jnp.arange(wf.shape[0])[:, None] % 256) // QKV_BLOCK
    c = jnp.arange(256)[None, :] // QKV_BLOCK
    return jnp.where(r == c, spread, 0.0).astype(BF16).reshape(-1, 256, 256)


def kernel(x, norm_g, w_in, w_gla_gate_up, b_gla_gate, gla_norm_g, conv_w, conv_b, w_q_m, w_k_m, w_v_m, w_igate, b_igate, w_fgate, b_fgate, mlstm_norm_g, mlstm_skip, w_out, final_norm_g):
    bsz, seq, d = x.shape
    assert d == D_MODEL and seq % TB == 0 and TB % TS == 0

    r0 = 2 * GLA_KEY_WIDTH + 2 * GLA_WIDTH
    w_gla = w_in[:, :r0].astype(BF16)
    w_ml = w_in[:, r0 + GLA_GATE_RANK:].astype(BF16)
    w_r = jnp.pad(w_in[:, r0:r0 + GLA_GATE_RANK], ((0, 0), (0, LANES - GLA_GATE_RANK))).astype(BF16)
    w_up = jnp.pad(w_gla_gate_up, ((0, LANES - GLA_GATE_RANK), (0, 0))).astype(BF16)
    bd_qk = jnp.concatenate([_block_diag_tiles(w_q_m), _block_diag_tiles(w_k_m)], axis=2)
    bd_v = _block_diag_tiles(w_v_m)
    w_gate = jnp.pad(jnp.concatenate([w_igate, w_fgate], axis=1),
                     ((0, 0), (0, LANES - 2 * MLSTM_HEADS))).astype(BF16)
    b_gate = jnp.pad(jnp.concatenate([b_igate, b_fgate]), (0, LANES - 2 * MLSTM_HEADS)).reshape(1, LANES)
    conv_w8 = jnp.pad(conv_w, ((0, SUBLANES - CONV_WIDTH), (0, 0)))
    row = lambda a: a.reshape(1, -1)

    operands = [
        x, row(norm_g), w_gla, w_ml, w_r, w_up, row(b_gla_gate), row(gla_norm_g), conv_w8, row(conv_b),
        bd_qk, bd_v, w_gate, b_gate, row(mlstm_norm_g), row(mlstm_skip), w_out.astype(BF16),
        row(final_norm_g),
        _tri(CG), _pair_level(CG, GLA_LEVELS), _level_exp2_scale(CG, GLA_LEVELS), _tri(TS),
    ]

    def const_spec(a):
        nd = a.ndim
        return pl.BlockSpec(a.shape, lambda b, t, _nd=nd: (0,) * _nd, pipeline_mode=pl.Buffered(1))

    tok_spec = pl.BlockSpec((None, TB, D_MODEL), lambda b, t: (b, t, 0))
    in_specs = [tok_spec] + [const_spec(a) for a in operands[1:]]

    scratch = [
        pltpu.VMEM((TS, D_MODEL), BF16),
        pltpu.VMEM((TS, GLA_KEY_WIDTH), F32),
        pltpu.VMEM((TS, GLA_KEY_WIDTH), F32),
        pltpu.VMEM((TS, GLA_WIDTH), BF16),
        pltpu.VMEM((TS, GLA_WIDTH), F32),
        pltpu.VMEM((TS, GLA_KEY_WIDTH), F32),
        pltpu.VMEM((CG, GLA_KEY_WIDTH), F32),
        pltpu.VMEM((HIST + TS, MLSTM_WIDTH), F32),
        pltpu.VMEM((TS, MLSTM_WIDTH), F32),
        pltpu.VMEM((TS, MLSTM_WIDTH), F32),
        pltpu.VMEM((TS, MLSTM_WIDTH), BF16),
        pltpu.VMEM((TS, MLSTM_WIDTH), BF16),
        pltpu.VMEM((TS, MLSTM_WIDTH), BF16),
        pltpu.VMEM((TS, GLA_WIDTH + MLSTM_WIDTH), BF16),
        pltpu.VMEM((GLA_HEADS, GLA_DV, GLA_DK), F32),
        pltpu.VMEM((MLSTM_HEADS, MLSTM_DH, MLSTM_DH), F32),
        pltpu.VMEM((MLSTM_HEADS, SUBLANES, MLSTM_DH), F32),
        pltpu.VMEM((MLSTM_HEADS, SUBLANES, LANES), F32),
    ]

    return pl.pallas_call(
        _layer_kernel,
        out_shape=jax.ShapeDtypeStruct(x.shape, x.dtype),
        grid=(bsz, seq // TB),
        in_specs=in_specs,
        out_specs=tok_spec,
        scratch_shapes=scratch,
        compiler_params=pltpu.CompilerParams(
            dimension_semantics=("arbitrary", "arbitrary"),
            vmem_limit_bytes=VMEM_LIMIT_BYTES),
        name="gla_mlstm_layer",
    )(*operands)
```

```python
import math

import jax
import jax.numpy as jnp
from jax import lax
from jax.experimental import pallas as pl
from jax.experimental.pallas import tpu as pltpu

F32 = jnp.float32
BF16 = jnp.bfloat16

D_MODEL = 1024
GLA_HEADS = 4
GLA_DK = 128
GLA_DV = 256
GLA_KEY_WIDTH = GLA_HEADS * GLA_DK
GLA_WIDTH = GLA_HEADS * GLA_DV
GLA_GATE_RANK = 16
GLA_GATE_TAU = 16.0
MLSTM_HEADS = 4
MLSTM_DH = 256
MLSTM_WIDTH = MLSTM_HEADS * MLSTM_DH
QKV_BLOCK = 4
CONV_WIDTH = 4
EPS = 1e-6

LANES = 128
SUBLANES = 8
TS = 256
TB = 1024
CG = 128
GLA_LEVELS = 7
HIST = SUBLANES
VMEM_LIMIT_BYTES = 60 * 1024 * 1024
LOG2E = math.log2(math.e)


def _dot(a, b):
    return jnp.dot(a, b, preferred_element_type=F32)


def _dot_nt(a, b):
    return lax.dot_general(a, b, (((1,), (1,)), ((), ())), preferred_element_type=F32)


def _dot_tn(a, b):
    return lax.dot_general(a, b, (((0,), (0,)), ((), ())), preferred_element_type=F32)


def _log_sigmoid(z):
    return jnp.minimum(z, 0.0) - jnp.log(1.0 + jnp.exp(-jnp.abs(z)))


def _silu(z):
    half = 0.5 * z
    return half * jnp.tanh(half) + half


def _split3(x):
    h1 = x.astype(BF16)
    r1 = x - h1.astype(F32)
    h2 = r1.astype(BF16)
    h3 = (r1 - h2.astype(F32)).astype(BF16)
    return h1, h2, h3


def _cumsum_rows(tri, x):
    h1, h2, h3 = _split3(x)
    return _dot(tri, h1) + _dot(tri, h2) + _dot(tri, h3)


def _cumsum_rows2(tri, x):
    h1 = x.astype(BF16)
    h2 = (x - h1.astype(F32)).astype(BF16)
    return _dot(tri, h1) + _dot(tri, h2)


def _tri(n):
    row = jnp.arange(n)[:, None]
    col = jnp.arange(n)[None, :]
    return jnp.where(row >= col, 1.0, 0.0).astype(BF16)


def _pair_level(n, levels):
    row = jnp.arange(n, dtype=jnp.int32)[:, None]
    col = jnp.arange(n, dtype=jnp.int32)[None, :]
    x = jnp.bitwise_xor(row, col)
    lvl = jnp.full((n, n), -1, jnp.int32)
    for p in range(levels):
        lvl = jnp.where(jnp.right_shift(x, p) == 1, p, lvl)
    return jnp.where(row > col, lvl, -1)


def _level_exp2_scale(n, levels):
    row = jnp.arange(n, dtype=jnp.int32)[None, :, None]
    p = jnp.arange(levels, dtype=jnp.int32)[:, None, None]
    upper = jnp.bitwise_and(jnp.right_shift(row, p), 1) == 1
    return jnp.broadcast_to(jnp.where(upper, LOG2E, -LOG2E), (levels, n, LANES)).astype(F32)


def _midpoint_rows(b_ref, lanes, p):
    s = 1 << (p + 1)
    half = s // 2
    if s >= SUBLANES:
        pieces = []
        for g in range(CG // s):
            row = g * s + half - 1
            pieces.append(jnp.broadcast_to(b_ref[row:row + 1, lanes], (s, LANES)))
        return pieces[0] if len(pieces) == 1 else jnp.concatenate(pieces, axis=0)
    assert s == 4
    sub = lax.broadcasted_iota(jnp.int32, (SUBLANES, LANES), 0)
    pieces = []
    for g in range(CG // SUBLANES):
        lo = jnp.broadcast_to(b_ref[g * 8 + 1:g * 8 + 2, lanes], (SUBLANES, LANES))
        hi = jnp.broadcast_to(b_ref[g * 8 + 5:g * 8 + 6, lanes], (SUBLANES, LANES))
        pieces.append(jnp.where(sub < 4, lo, hi))
    return jnp.concatenate(pieces, axis=0)


def _layer_kernel(x_ref, ng_ref, wgla_ref, wml_ref, wr_ref, wup_ref, bg_ref, glag_ref, cw_ref, cb_ref,
                  bdqk_ref, bdv_ref, wgate_ref, bgate_ref, mng_ref, skip_ref, wout_ref, fng_ref,
                  trig_ref, lvl_ref, escale_ref, trim_ref,
                  out_ref,
                  u_s, qg_s, kg_s, vg_s, zg_s, la_s, b_s, xm_s, zm_s, cact_s, qm_s, km_s, vm_s,
                  mix_s, sg_s, cm_s, nm_s, mm_s, y_s):
    @pl.when(pl.program_id(1) == 0)
    def _reset_state():
        sg_s[...] = jnp.zeros_like(sg_s)
        cm_s[...] = jnp.zeros_like(cm_s)
        nm_s[...] = jnp.zeros_like(nm_s)
        mm_s[...] = jnp.zeros_like(mm_s)
        xm_s[0:HIST, :] = jnp.zeros((HIST, MLSTM_WIDTH), F32)

    @pl.when(jnp.logical_and(pl.program_id(0) == 0, pl.program_id(1) == 0))
    def _init_pending_output():
        y_s[...] = jnp.zeros_like(y_s)

    _pre_norm(x_ref, 0, ng_ref, u_s)
    n_tiles = TB // TS

    def tile(i, carry):
        _process_tile(pl.multiple_of(i * TS, TS),
                      pl.multiple_of(jnp.maximum(i - 1, 0) * TS, TS),
                      pl.multiple_of(jnp.minimum(i + 1, n_tiles - 1) * TS, TS),
                      x_ref, ng_ref, wgla_ref, wml_ref, wr_ref, wup_ref, bg_ref, glag_ref, cw_ref, cb_ref,
                      bdqk_ref, bdv_ref, wgate_ref, bgate_ref, mng_ref, skip_ref, wout_ref, fng_ref,
                      trig_ref, lvl_ref, escale_ref, trim_ref, out_ref,
                      u_s, qg_s, kg_s, vg_s, zg_s, la_s, b_s, xm_s, zm_s, cact_s, qm_s, km_s, vm_s,
                      mix_s, sg_s, cm_s, nm_s, mm_s, y_s)
        return carry

    lax.fori_loop(0, n_tiles, tile, 0)
    _final_norm(y_s, fng_ref, out_ref, TB - TS)


def _pre_norm(x_ref, row0, ng_ref, u_s):
    x = x_ref[pl.ds(row0, TS), :]
    ms = jnp.mean(x * x, axis=-1, keepdims=True)
    u_s[...] = (x * lax.rsqrt(ms + EPS) * ng_ref[...]).astype(BF16)


def _final_norm(y_s, fng_ref, out_ref, row0):
    y = y_s[...]
    out_ref[pl.ds(row0, TS), :] = (y * lax.rsqrt(jnp.mean(y * y, axis=-1, keepdims=True) + EPS)
                                   * fng_ref[...])


def _process_tile(row0, row_prev, row_next,
                  x_ref, ng_ref, wgla_ref, wml_ref, wr_ref, wup_ref, bg_ref, glag_ref, cw_ref, cb_ref,
                  bdqk_ref, bdv_ref, wgate_ref, bgate_ref, mng_ref, skip_ref, wout_ref, fng_ref,
                  trig_ref, lvl_ref, escale_ref, trim_ref, out_ref,
                  u_s, qg_s, kg_s, vg_s, zg_s, la_s, b_s, xm_s, zm_s, cact_s, qm_s, km_s, vm_s,
                  mix_s, sg_s, cm_s, nm_s, mm_s, y_s):
    u = u_s[...]
    r_g = _dot(u, wr_ref[...]).astype(BF16)
    gate = _dot(r_g, wup_ref[...]) + bg_ref[...]
    la_s[...] = _log_sigmoid(gate) * (1.0 / GLA_GATE_TAU)
    _final_norm(y_s, fng_ref, out_ref, row_prev)
    c0 = 0
    qg_s[...] = _dot(u, wgla_ref[:, c0:c0 + GLA_KEY_WIDTH]) * (GLA_DK ** -0.5)
    c0 += GLA_KEY_WIDTH
    kg_s[...] = _dot(u, wgla_ref[:, c0:c0 + GLA_KEY_WIDTH])
    c0 += GLA_KEY_WIDTH
    vg_s[...] = _dot(u, wgla_ref[:, c0:c0 + GLA_WIDTH]).astype(BF16)
    c0 += GLA_WIDTH
    zg_s[...] = _dot(u, wgla_ref[:, c0:c0 + GLA_WIDTH])
    xm_s[HIST:HIST + TS, :] = _dot(u, wml_ref[:, 0:MLSTM_WIDTH])
    zm_s[...] = _dot(u, wml_ref[:, MLSTM_WIDTH:2 * MLSTM_WIDTH])

    front = {}

    def front_conv():
        conv = cb_ref[...] + cw_ref[3:4, :] * xm_s[HIST:HIST + TS, :]
        for w in range(CONV_WIDTH - 1):
            off = HIST - (CONV_WIDTH - 1) + w
            conv = conv + cw_ref[w:w + 1, :] * xm_s[off:off + TS, :]
        cact_s[...] = _silu(conv)

    def front_headwise(j):
        cl = slice(j * 256, (j + 1) * 256)
        qk = _dot(cact_s[:, cl].astype(BF16), bdqk_ref[j])
        qm_s[:, cl] = qk[:, 0:256].astype(BF16)
        km_s[:, cl] = qk[:, 256:512].astype(BF16)
        vm_s[:, cl] = _dot(xm_s[HIST:HIST + TS, cl].astype(BF16), bdv_ref[j]).astype(BF16)

    def front_gates():
        xm_s[0:HIST, :] = xm_s[TS:TS + HIST, :]
        front["g"] = (_dot(qm_s[...], wgate_ref[0:MLSTM_WIDTH, :])
                      + _dot(km_s[...], wgate_ref[MLSTM_WIDTH:2 * MLSTM_WIDTH, :])
                      + _dot(vm_s[...], wgate_ref[2 * MLSTM_WIDTH:3 * MLSTM_WIDTH, :])
                      + bgate_ref[...])

    def front_decay():
        g = front["g"]
        bcum = _cumsum_rows(trim_ref[...], _log_sigmoid(g))
        bcum = pltpu.roll(bcum, LANES - MLSTM_HEADS, axis=1)
        front["bcum"] = bcum
        front["a_all"] = g - bcum
        front["a_all_t"] = front["a_all"].T

    front_pieces = [front_conv,
                    lambda: (front_headwise(0), front_headwise(1)),
                    lambda: (front_headwise(2), front_headwise(3)),
                    front_gates, front_decay]

    lvl = lvl_ref[...]
    for c in range(TS // CG):
        rows = slice(c * CG, (c + 1) * CG)
        b_s[...] = _cumsum_rows2(trig_ref[...], la_s[rows, :])
        for h in range(GLA_HEADS):
            kl = slice(h * GLA_DK, (h + 1) * GLA_DK)
            vl = slice(h * GLA_DV, (h + 1) * GLA_DV)
            q = qg_s[rows, kl]
            k = kg_s[rows, kl]
            b = b_s[:, kl]
            q16 = q.astype(BF16)
            k16 = k.astype(BF16)
            e0 = jnp.exp(la_s[rows, kl]).astype(BF16)
            sc = jnp.where(lvl == 0, _dot_nt(q16 * e0, k16), 0.0)
            for p in range(1, GLA_LEVELS):
                e = jnp.exp2((b - _midpoint_rows(b_s, kl, p)) * escale_ref[p]).astype(BF16)
                sc = jnp.where(lvl == p, _dot_nt(q16 * e, k16 * e), sc)
            v = vg_s[rows, vl]
            st = sg_s[h]
            diag = jnp.sum(q * k, axis=-1, keepdims=True)
            o = (_dot(sc.astype(BF16), v) + diag * v.astype(F32)
                 + _dot_nt((q * jnp.exp(b)).astype(BF16), st.astype(BF16)))
            b_last = b[CG - 1:CG, :]
            k_dec = (k * jnp.exp(b_last - b)).astype(BF16)
            sg_s[h] = st * jnp.exp(b_last) + _dot_tn(v, k_dec)
            o = o * lax.rsqrt(jnp.mean(o * o, axis=-1, keepdims=True) + EPS) * glag_ref[:, vl]
            mix_s[rows, vl] = (o * _silu(zg_s[rows, vl])).astype(BF16)
            if front_pieces:
                front_pieces.pop(0)()

    assert not front_pieces
    bcum, a_all, a_all_t = front["bcum"], front["a_all"], front["a_all_t"]
    causal = (lax.broadcasted_iota(jnp.int32, (TS, TS), 0)
              >= lax.broadcasted_iota(jnp.int32, (TS, TS), 1))
    log_kscale = 0.5 * math.log(MLSTM_DH)
    for h in range(MLSTM_HEADS):
        hl = slice(h * MLSTM_DH, (h + 1) * MLSTM_DH)
        a_row = a_all_t[h:h + 1, :]
        a_col = a_all[:, h:h + 1]
        b_col = bcum[:, h:h + 1]
        m_prev = mm_s[h][0:1, 0:1]
        run_max = jnp.max(jnp.where(causal, a_row, -jnp.inf), axis=-1, keepdims=True)
        m_rel = jnp.maximum(m_prev, run_max)
        dmat = jnp.exp(jnp.where(causal, (a_row - log_kscale) - m_rel, -jnp.inf))
        q = qm_s[:, hl]
        k = km_s[:, hl]
        v = vm_s[:, hl]
        s = _dot_nt(q, k) * dmat
        w_inter = jnp.exp(m_prev - m_rel)
        ct = cm_s[h]
        n_row = nm_s[h][0:1, :]
        num = _dot(s.astype(BF16), v) + w_inter * _dot(q, ct.astype(BF16))
        den = (jnp.sum(s, axis=-1, keepdims=True)
               + w_inter * jnp.sum(q.astype(F32) * n_row, axis=-1, keepdims=True))
        m_abs = b_col + m_rel
        hh = num * (1.0 / jnp.maximum(jnp.abs(den), jnp.exp(-m_abs)))
        hh = hh * lax.rsqrt(jnp.mean(hh * hh, axis=-1, keepdims=True) + EPS) * mng_ref[:, hl]
        o = (hh + skip_ref[:, hl] * cact_s[:, hl]) * _silu(zm_s[:, hl])
        mix_s[:, GLA_WIDTH + h * MLSTM_DH:GLA_WIDTH + (h + 1) * MLSTM_DH] = o.astype(BF16)
        m_last = m_rel[TS - 1:TS, :]
        decay = jnp.exp(m_prev - m_last)
        wk_col = jnp.exp((a_col - log_kscale) - m_last)
        wk_row = jnp.exp((a_row - log_kscale) - m_last)
        vw = (v.astype(F32) * wk_col).astype(BF16)
        cm_s[h] = decay * ct + _dot_tn(k, vw)
        n_add = _dot(jnp.broadcast_to(wk_row, (SUBLANES, TS)).astype(BF16), k)
        nm_s[h] = decay * nm_s[h] + n_add
        mm_s[h] = jnp.broadcast_to(b_col[TS - 1:TS, :] + m_last, (SUBLANES, LANES))

    y_s[...] = x_ref[pl.ds(row0, TS), :] + _dot(mix_s[...], wout_ref[...])
    _pre_norm(x_ref, row_next, ng_ref, u_s)


def _block_diag_tiles(w):
    wf = w.reshape(-1, QKV_BLOCK).astype(BF16)
    sel = (jnp.arange(256)[None, :] % QKV_BLOCK == jnp.arange(QKV_BLOCK)[:, None]).astype(BF16)
    spread = jnp.dot(wf, sel, preferred_element_type=F32)
    r = (jnp.arange(wf.shape[0])[:, None] % 256) // QKV_BLOCK
    c = jnp.arange(256)[None, :] // QKV_BLOCK
    return jnp.where(r == c, spread, 0.0).astype(BF16).reshape(-1, 256, 256)


def kernel(x, norm_g, w_in, w_gla_gate_up, b_gla_gate, gla_norm_g, conv_w, conv_b, w_q_m, w_k_m, w_v_m, w_igate, b_igate, w_fgate, b_fgate, mlstm_norm_g, mlstm_skip, w_out, final_norm_g):
    bsz, seq, d = x.shape
    assert d == D_MODEL and seq % TB == 0 and TB % TS == 0

    r0 = 2 * GLA_KEY_WIDTH + 2 * GLA_WIDTH
    w_gla = w_in[:, :r0].astype(BF16)
    w_ml = w_in[:, r0 + GLA_GATE_RANK:].astype(BF16)
    w_r = jnp.pad(w_in[:, r0:r0 + GLA_GATE_RANK], ((0, 0), (0, LANES - GLA_GATE_RANK))).astype(BF16)
    w_up = jnp.pad(w_gla_gate_up, ((0, LANES - GLA_GATE_RANK), (0, 0))).astype(BF16)
    bd_qk = jnp.concatenate([_block_diag_tiles(w_q_m), _block_diag_tiles(w_k_m)], axis=2)
    bd_v = _block_diag_tiles(w_v_m)
    w_gate = jnp.pad(jnp.concatenate([w_igate, w_fgate], axis=1),
                     ((0, 0), (0, LANES - 2 * MLSTM_HEADS))).astype(BF16)
    b_gate = jnp.pad(jnp.concatenate([b_igate, b_fgate]), (0, LANES - 2 * MLSTM_HEADS)).reshape(1, LANES)
    conv_w8 = jnp.pad(conv_w, ((0, SUBLANES - CONV_WIDTH), (0, 0)))
    row = lambda a: a.reshape(1, -1)

    operands = [
        x, row(norm_g), w_gla, w_ml, w_r, w_up, row(b_gla_gate), row(gla_norm_g), conv_w8, row(conv_b),
        bd_qk, bd_v, w_gate, b_gate, row(mlstm_norm_g), row(mlstm_skip), w_out.astype(BF16),
        row(final_norm_g),
        _tri(CG), _pair_level(CG, GLA_LEVELS), _level_exp2_scale(CG, GLA_LEVELS), _tri(TS),
    ]

    def const_spec(a):
        nd = a.ndim
        return pl.BlockSpec(a.shape, lambda b, t, _nd=nd: (0,) * _nd, pipeline_mode=pl.Buffered(1))

    tok_spec = pl.BlockSpec((None, TB, D_MODEL), lambda b, t: (b, t, 0))
    in_specs = [tok_spec] + [const_spec(a) for a in operands[1:]]

    scratch = [
        pltpu.VMEM((TS, D_MODEL), BF16),
        pltpu.VMEM((TS, GLA_KEY_WIDTH), F32),
        pltpu.VMEM((TS, GLA_KEY_WIDTH), F32),
        pltpu.VMEM((TS, GLA_WIDTH), BF16),
        pltpu.VMEM((TS, GLA_WIDTH), F32),
        pltpu.VMEM((TS, GLA_KEY_WIDTH), F32),
        pltpu.VMEM((CG, GLA_KEY_WIDTH), F32),
        pltpu.VMEM((HIST + TS, MLSTM_WIDTH), F32),
        pltpu.VMEM((TS, MLSTM_WIDTH), F32),
        pltpu.VMEM((TS, MLSTM_WIDTH), F32),
        pltpu.VMEM((TS, MLSTM_WIDTH), BF16),
        pltpu.VMEM((TS, MLSTM_WIDTH), BF16),
        pltpu.VMEM((TS, MLSTM_WIDTH), BF16),
        pltpu.VMEM((TS, GLA_WIDTH + MLSTM_WIDTH), BF16),
        pltpu.VMEM((GLA_HEADS, GLA_DV, GLA_DK), F32),
        pltpu.VMEM((MLSTM_HEADS, MLSTM_DH, MLSTM_DH), F32),
        pltpu.VMEM((MLSTM_HEADS, SUBLANES, MLSTM_DH), F32),
        pltpu.VMEM((MLSTM_HEADS, SUBLANES, LANES), F32),
        pltpu.VMEM((TS, D_MODEL), F32),
    ]

    return pl.pallas_call(
        _layer_kernel,
        out_shape=jax.ShapeDtypeStruct(x.shape, x.dtype),
        grid=(bsz, seq // TB),
        in_specs=in_specs,
        out_specs=tok_spec,
        scratch_shapes=scratch,
        compiler_params=pltpu.CompilerParams(
            dimension_semantics=("arbitrary", "arbitrary"),
            vmem_limit_bytes=VMEM_LIMIT_BYTES),
        name="gla_mlstm_layer",
    )(*operands)
```

```python
import math

import jax
import jax.numpy as jnp
from jax import lax
from jax.experimental import pallas as pl
from jax.experimental.pallas import tpu as pltpu

F32 = jnp.float32
BF16 = jnp.bfloat16

D_MODEL = 1024
GLA_HEADS = 4
GLA_DK = 128
GLA_DV = 256
GLA_KEY_WIDTH = GLA_HEADS * GLA_DK
GLA_WIDTH = GLA_HEADS * GLA_DV
GLA_GATE_RANK = 16
GLA_GATE_TAU = 16.0
MLSTM_HEADS = 4
MLSTM_DH = 256
MLSTM_WIDTH = MLSTM_HEADS * MLSTM_DH
QKV_BLOCK = 4
CONV_WIDTH = 4
EPS = 1e-6

LANES = 128
SUBLANES = 8
TS = 256
TB = 1024
CG = 128
GLA_LEVELS = 7
HIST = SUBLANES
VMEM_LIMIT_BYTES = 60 * 1024 * 1024
LOG2E = math.log2(math.e)


def _dot(a, b):
    return jnp.dot(a, b, preferred_element_type=F32)


def _dot_nt(a, b):
    return lax.dot_general(a, b, (((1,), (1,)), ((), ())), preferred_element_type=F32)


def _dot_tn(a, b):
    return lax.dot_general(a, b, (((0,), (0,)), ((), ())), preferred_element_type=F32)


def _log_sigmoid(z):
    return jnp.minimum(z, 0.0) - jnp.log(1.0 + jnp.exp(-jnp.abs(z)))


def _silu(z):
    half = 0.5 * z
    return half * jnp.tanh(half) + half


def _split3(x):
    h1 = x.astype(BF16)
    r1 = x - h1.astype(F32)
    h2 = r1.astype(BF16)
    h3 = (r1 - h2.astype(F32)).astype(BF16)
    return h1, h2, h3


def _cumsum_rows(tri, x):
    h1, h2, h3 = _split3(x)
    return _dot(tri, h1) + _dot(tri, h2) + _dot(tri, h3)


def _cumsum_rows2(tri, x):
    h1 = x.astype(BF16)
    h2 = (x - h1.astype(F32)).astype(BF16)
    return _dot(tri, h1) + _dot(tri, h2)


def _tri(n):
    row = jnp.arange(n)[:, None]
    col = jnp.arange(n)[None, :]
    return jnp.where(row >= col, 1.0, 0.0).astype(BF16)


def _pair_level(n, levels):
    row = jnp.arange(n, dtype=jnp.int32)[:, None]
    col = jnp.arange(n, dtype=jnp.int32)[None, :]
    x = jnp.bitwise_xor(row, col)
    lvl = jnp.full((n, n), -1, jnp.int32)
    for p in range(levels):
        lvl = jnp.where(jnp.right_shift(x, p) == 1, p, lvl)
    return jnp.where(row > col, lvl, -1)


def _level_exp2_scale(n, levels):
    row = jnp.arange(n, dtype=jnp.int32)[None, :, None]
    p = jnp.arange(levels, dtype=jnp.int32)[:, None, None]
    upper = jnp.bitwise_and(jnp.right_shift(row, p), 1) == 1
    return jnp.broadcast_to(jnp.where(upper, LOG2E, -LOG2E), (levels, n, LANES)).astype(F32)


def _midpoint_rows(b_ref, lanes, p):
    s = 1 << (p + 1)
    half = s // 2
    if s >= SUBLANES:
        pieces = []
        for g in range(CG // s):
            row = g * s + half - 1
            pieces.append(jnp.broadcast_to(b_ref[row:row + 1, lanes], (s, LANES)))
        return pieces[0] if len(pieces) == 1 else jnp.concatenate(pieces, axis=0)
    assert s == 4
    sub = lax.broadcasted_iota(jnp.int32, (SUBLANES, LANES), 0)
    pieces = []
    for g in range(CG // SUBLANES):
        lo = jnp.broadcast_to(b_ref[g * 8 + 1:g * 8 + 2, lanes], (SUBLANES, LANES))
        hi = jnp.broadcast_to(b_ref[g * 8 + 5:g * 8 + 6, lanes], (SUBLANES, LANES))
        pieces.append(jnp.where(sub < 4, lo, hi))
    return jnp.concatenate(pieces, axis=0)


def _layer_kernel(x_ref, ng_ref, wgla_ref, wml_ref, wr_ref, wup_ref, bg_ref, glag_ref, cw_ref, cb_ref,
                  bd_ref, wgate_ref, bgate_ref, mng_ref, skip_ref, wout_ref, fng_ref,
                  trig_ref, lvl_ref, escale_ref, trim_ref,
                  out_ref,
                  u_s, qg_s, kg_s, vg_s, zg_s, la_s, b_s, xm_s, zm_s, cact_s, qm_s, km_s, vm_s,
                  mix_s, sg_s, cm_s, nm_s, mm_s):
    @pl.when(pl.program_id(1) == 0)
    def _reset_state():
        sg_s[...] = jnp.zeros_like(sg_s)
        cm_s[...] = jnp.zeros_like(cm_s)
        nm_s[...] = jnp.zeros_like(nm_s)
        mm_s[...] = jnp.zeros_like(mm_s)
        xm_s[0:HIST, :] = jnp.zeros((HIST, MLSTM_WIDTH), F32)

    def tile(i, carry):
        _process_tile(pl.multiple_of(i * TS, TS),
                      x_ref, ng_ref, wgla_ref, wml_ref, wr_ref, wup_ref, bg_ref, glag_ref, cw_ref, cb_ref,
                      bd_ref, wgate_ref, bgate_ref, mng_ref, skip_ref, wout_ref, fng_ref,
                      trig_ref, lvl_ref, escale_ref, trim_ref, out_ref,
                      u_s, qg_s, kg_s, vg_s, zg_s, la_s, b_s, xm_s, zm_s, cact_s, qm_s, km_s, vm_s,
                      mix_s, sg_s, cm_s, nm_s, mm_s)
        return carry

    lax.fori_loop(0, TB // TS, tile, 0)


def _process_tile(row0, x_ref, ng_ref, wgla_ref, wml_ref, wr_ref, wup_ref, bg_ref, glag_ref, cw_ref, cb_ref,
                  bd_ref, wgate_ref, bgate_ref, mng_ref, skip_ref, wout_ref, fng_ref,
                  trig_ref, lvl_ref, escale_ref, trim_ref, out_ref,
                  u_s, qg_s, kg_s, vg_s, zg_s, la_s, b_s, xm_s, zm_s, cact_s, qm_s, km_s, vm_s,
                  mix_s, sg_s, cm_s, nm_s, mm_s):
    x = x_ref[pl.ds(row0, TS), :]
    ms = jnp.mean(x * x, axis=-1, keepdims=True)
    u_s[...] = (x * lax.rsqrt(ms + EPS) * ng_ref[...]).astype(BF16)
    u = u_s[...]
    r_g = _dot(u, wr_ref[...]).astype(BF16)
    gate = _dot(r_g, wup_ref[...]) + bg_ref[...]
    la_s[...] = _log_sigmoid(gate) * (1.0 / GLA_GATE_TAU)
    c0 = 0
    qg_s[...] = _dot(u, wgla_ref[:, c0:c0 + GLA_KEY_WIDTH]) * (GLA_DK ** -0.5)
    c0 += GLA_KEY_WIDTH
    kg_s[...] = _dot(u, wgla_ref[:, c0:c0 + GLA_KEY_WIDTH])
    c0 += GLA_KEY_WIDTH
    vg_s[...] = _dot(u, wgla_ref[:, c0:c0 + GLA_WIDTH]).astype(BF16)
    c0 += GLA_WIDTH
    zg_s[...] = _dot(u, wgla_ref[:, c0:c0 + GLA_WIDTH])
    xm_s[HIST:HIST + TS, :] = _dot(u, wml_ref[:, 0:MLSTM_WIDTH])
    zm_s[...] = _dot(u, wml_ref[:, MLSTM_WIDTH:2 * MLSTM_WIDTH])

    front = {}

    def front_conv():
        conv = cb_ref[...] + cw_ref[3:4, :] * xm_s[HIST:HIST + TS, :]
        for w in range(CONV_WIDTH - 1):
            off = HIST - (CONV_WIDTH - 1) + w
            conv = conv + cw_ref[w:w + 1, :] * xm_s[off:off + TS, :]
        cact_s[...] = _silu(conv)

    def front_headwise(j):
        cl = slice(j * 256, (j + 1) * 256)
        c_act = cact_s[:, cl].astype(BF16)
        qm_s[:, cl] = _dot(c_act, bd_ref[0, j]).astype(BF16)
        km_s[:, cl] = _dot(c_act, bd_ref[1, j]).astype(BF16)
        vm_s[:, cl] = _dot(xm_s[HIST:HIST + TS, cl].astype(BF16), bd_ref[2, j]).astype(BF16)

    def front_gates():
        xm_s[0:HIST, :] = xm_s[TS:TS + HIST, :]
        front["g"] = (_dot(qm_s[...], wgate_ref[0:MLSTM_WIDTH, :])
                      + _dot(km_s[...], wgate_ref[MLSTM_WIDTH:2 * MLSTM_WIDTH, :])
                      + _dot(vm_s[...], wgate_ref[2 * MLSTM_WIDTH:3 * MLSTM_WIDTH, :])
                      + bgate_ref[...])

    def front_decay():
        g = front["g"]
        bcum = _cumsum_rows(trim_ref[...], _log_sigmoid(g))
        bcum = pltpu.roll(bcum, LANES - MLSTM_HEADS, axis=1)
        front["bcum"] = bcum
        front["a_all"] = g - bcum
        front["a_all_t"] = front["a_all"].T

    front_pieces = [front_conv,
                    lambda: (front_headwise(0), front_headwise(1)),
                    lambda: (front_headwise(2), front_headwise(3)),
                    front_gates, front_decay]

    lvl = lvl_ref[...]
    for c in range(TS // CG):
        rows = slice(c * CG, (c + 1) * CG)
        b_s[...] = _cumsum_rows2(trig_ref[...], la_s[rows, :])
        for h in range(GLA_HEADS):
            kl = slice(h * GLA_DK, (h + 1) * GLA_DK)
            vl = slice(h * GLA_DV, (h + 1) * GLA_DV)
            q = qg_s[rows, kl]
            k = kg_s[rows, kl]
            b = b_s[:, kl]
            q16 = q.astype(BF16)
            k16 = k.astype(BF16)
            e0 = jnp.exp(la_s[rows, kl]).astype(BF16)
            sc = jnp.where(lvl == 0, _dot_nt(q16 * e0, k16), 0.0)
            for p in range(1, GLA_LEVELS):
                e = jnp.exp2((b - _midpoint_rows(b_s, kl, p)) * escale_ref[p]).astype(BF16)
                sc = jnp.where(lvl == p, _dot_nt(q16 * e, k16 * e), sc)
            v = vg_s[rows, vl]
            st = sg_s[h]
            diag = jnp.sum(q * k, axis=-1, keepdims=True)
            o = (_dot(sc.astype(BF16), v) + diag * v.astype(F32)
                 + _dot_nt((q * jnp.exp(b)).astype(BF16), st.astype(BF16)))
            b_last = b[CG - 1:CG, :]
            k_dec = (k * jnp.exp(b_last - b)).astype(BF16)
            sg_s[h] = st * jnp.exp(b_last) + _dot_tn(v, k_dec)
            o = o * lax.rsqrt(jnp.mean(o * o, axis=-1, keepdims=True) + EPS) * glag_ref[:, vl]
            mix_s[rows, vl] = (o * _silu(zg_s[rows, vl])).astype(BF16)
            if front_pieces:
                front_pieces.pop(0)()

    assert not front_pieces
    bcum, a_all, a_all_t = front["bcum"], front["a_all"], front["a_all_t"]
    causal = (lax.broadcasted_iota(jnp.int32, (TS, TS), 0)
              >= lax.broadcasted_iota(jnp.int32, (TS, TS), 1))
    log_kscale = 0.5 * math.log(MLSTM_DH)
    for h in range(MLSTM_HEADS):
        hl = slice(h * MLSTM_DH, (h + 1) * MLSTM_DH)
        a_row = a_all_t[h:h + 1, :]
        a_col = a_all[:, h:h + 1]
        b_col = bcum[:, h:h + 1]
        m_prev = mm_s[h][0:1, 0:1]
        run_max = jnp.max(jnp.where(causal, a_row, -jnp.inf), axis=-1, keepdims=True)
        m_rel = jnp.maximum(m_prev, run_max)
        dmat = jnp.exp(jnp.where(causal, (a_row - log_kscale) - m_rel, -jnp.inf))
        q = qm_s[:, hl]
        k = km_s[:, hl]
        v = vm_s[:, hl]
        s = _dot_nt(q, k) * dmat
        w_inter = jnp.exp(m_prev - m_rel)
        ct = cm_s[h]
        n_row = nm_s[h][0:1, :]
        num = _dot(s.astype(BF16), v) + w_inter * _dot(q, ct.astype(BF16))
        den = (jnp.sum(s, axis=-1, keepdims=True)
               + w_inter * jnp.sum(q.astype(F32) * n_row, axis=-1, keepdims=True))
        m_abs = b_col + m_rel
        hh = num * (1.0 / jnp.maximum(jnp.abs(den), jnp.exp(-m_abs)))
        hh = hh * lax.rsqrt(jnp.mean(hh * hh, axis=-1, keepdims=True) + EPS) * mng_ref[:, hl]
        o = (hh + skip_ref[:, hl] * cact_s[:, hl]) * _silu(zm_s[:, hl])
        mix_s[:, GLA_WIDTH + h * MLSTM_DH:GLA_WIDTH + (h + 1) * MLSTM_DH] = o.astype(BF16)
        m_last = m_rel[TS - 1:TS, :]
        decay = jnp.exp(m_prev - m_last)
        wk_col = jnp.exp((a_col - log_kscale) - m_last)
        wk_row = jnp.exp((a_row - log_kscale) - m_last)
        vw = (v.astype(F32) * wk_col).astype(BF16)
        cm_s[h] = decay * ct + _dot_tn(k, vw)
        n_add = _dot(jnp.broadcast_to(wk_row, (SUBLANES, TS)).astype(BF16), k)
        nm_s[h] = decay * nm_s[h] + n_add
        mm_s[h] = jnp.broadcast_to(b_col[TS - 1:TS, :] + m_last, (SUBLANES, LANES))

    y = x_ref[pl.ds(row0, TS), :] + _dot(mix_s[...], wout_ref[...])
    out_ref[pl.ds(row0, TS), :] = (y * lax.rsqrt(jnp.mean(y * y, axis=-1, keepdims=True) + EPS)
                                   * fng_ref[...])


def _block_diag_tiles(ws):
    wf = jnp.stack(ws).reshape(-1, QKV_BLOCK).astype(BF16)
    sel = (jnp.arange(256)[None, :] % QKV_BLOCK == jnp.arange(QKV_BLOCK)[:, None]).astype(BF16)
    spread = jnp.dot(wf, sel, preferred_element_type=F32)
    r = (jnp.arange(wf.shape[0])[:, None] % 256) // QKV_BLOCK
    c = jnp.arange(256)[None, :] // QKV_BLOCK
    return jnp.where(r == c, spread, 0.0).astype(BF16).reshape(len(ws), -1, 256, 256)


def _prep_w_in_kernel(w_ref, gla_ref, ml_ref, r_ref):
    r0 = 2 * GLA_KEY_WIDTH + 2 * GLA_WIDTH
    w = w_ref[...]
    gla_ref[...] = w[:, 0:r0].astype(BF16)
    ml_ref[...] = w[:, r0 + GLA_GATE_RANK:r0 + GLA_GATE_RANK + 2 * MLSTM_WIDTH].astype(BF16)
    lane = lax.broadcasted_iota(jnp.int32, (w.shape[0], LANES), 1)
    r_ref[...] = jnp.where(lane < GLA_GATE_RANK, w[:, r0:r0 + LANES], 0.0).astype(BF16)


def _prep_w_in(w_in):
    rows = 128
    d, width = w_in.shape
    r0 = 2 * GLA_KEY_WIDTH + 2 * GLA_WIDTH
    return pl.pallas_call(
        _prep_w_in_kernel,
        out_shape=(jax.ShapeDtypeStruct((d, r0), BF16), jax.ShapeDtypeStruct((d, 2 * MLSTM_WIDTH), BF16),
                   jax.ShapeDtypeStruct((d, LANES), BF16)),
        grid=(d // rows,),
        in_specs=[pl.BlockSpec((rows, width), lambda i: (i, 0))],
        out_specs=(pl.BlockSpec((rows, r0), lambda i: (i, 0)), pl.BlockSpec((rows, 2 * MLSTM_WIDTH), lambda i: (i, 0)),
                   pl.BlockSpec((rows, LANES), lambda i: (i, 0))),
        name="prep_w_in",
    )(w_in)


def kernel(x, norm_g, w_in, w_gla_gate_up, b_gla_gate, gla_norm_g, conv_w, conv_b, w_q_m, w_k_m, w_v_m, w_igate, b_igate, w_fgate, b_fgate, mlstm_norm_g, mlstm_skip, w_out, final_norm_g):
    bsz, seq, d = x.shape
    assert d == D_MODEL and seq % TB == 0 and TB % TS == 0

    r0 = 2 * GLA_KEY_WIDTH + 2 * GLA_WIDTH
    w_gla, w_ml, w_r = _prep_w_in(w_in)
    w_up = jnp.pad(w_gla_gate_up, ((0, LANES - GLA_GATE_RANK), (0, 0))).astype(BF16)
    bd = _block_diag_tiles([w_q_m, w_k_m, w_v_m])
    w_gate = jnp.pad(jnp.concatenate([w_igate, w_fgate], axis=1),
                     ((0, 0), (0, LANES - 2 * MLSTM_HEADS))).astype(BF16)
    b_gate = jnp.pad(jnp.concatenate([b_igate, b_fgate]), (0, LANES - 2 * MLSTM_HEADS)).reshape(1, LANES)
    conv_w8 = jnp.pad(conv_w, ((0, SUBLANES - CONV_WIDTH), (0, 0)))
    row = lambda a: a.reshape(1, -1)

    operands = [
        x, row(norm_g), w_gla, w_ml, w_r, w_up, row(b_gla_gate), row(gla_norm_g), conv_w8, row(conv_b),
        bd, w_gate, b_gate, row(mlstm_norm_g), row(mlstm_skip), w_out.astype(BF16),
        row(final_norm_g),
        _tri(CG), _pair_level(CG, GLA_LEVELS), _level_exp2_scale(CG, GLA_LEVELS), _tri(TS),
    ]

    def const_spec(a):
        nd = a.ndim
        return pl.BlockSpec(a.shape, lambda b, t, _nd=nd: (0,) * _nd, pipeline_mode=pl.Buffered(1))

    tok_spec = pl.BlockSpec((None, TB, D_MODEL), lambda b, t: (b, t, 0))
    in_specs = [tok_spec] + [const_spec(a) for a in operands[1:]]

    scratch = [
        pltpu.VMEM((TS, D_MODEL), BF16),
        pltpu.VMEM((TS, GLA_KEY_WIDTH), F32),
        pltpu.VMEM((TS, GLA_KEY_WIDTH), F32),
        pltpu.VMEM((TS, GLA_WIDTH), BF16),
        pltpu.VMEM((TS, GLA_WIDTH), F32),
        pltpu.VMEM((TS, GLA_KEY_WIDTH), F32),
        pltpu.VMEM((CG, GLA_KEY_WIDTH), F32),
        pltpu.VMEM((HIST + TS, MLSTM_WIDTH), F32),
        pltpu.VMEM((TS, MLSTM_WIDTH), F32),
        pltpu.VMEM((TS, MLSTM_WIDTH), F32),
        pltpu.VMEM((TS, MLSTM_WIDTH), BF16),
        pltpu.VMEM((TS, MLSTM_WIDTH), BF16),
        pltpu.VMEM((TS, MLSTM_WIDTH), BF16),
        pltpu.VMEM((TS, GLA_WIDTH + MLSTM_WIDTH), BF16),
        pltpu.VMEM((GLA_HEADS, GLA_DV, GLA_DK), F32),
        pltpu.VMEM((MLSTM_HEADS, MLSTM_DH, MLSTM_DH), F32),
        pltpu.VMEM((MLSTM_HEADS, SUBLANES, MLSTM_DH), F32),
        pltpu.VMEM((MLSTM_HEADS, SUBLANES, LANES), F32),
    ]

    return pl.pallas_call(
        _layer_kernel,
        out_shape=jax.ShapeDtypeStruct(x.shape, x.dtype),
        grid=(bsz, seq // TB),
        in_specs=in_specs,
        out_specs=tok_spec,
        scratch_shapes=scratch,
        compiler_params=pltpu.CompilerParams(
            dimension_semantics=("arbitrary", "arbitrary"),
            vmem_limit_bytes=VMEM_LIMIT_BYTES),
        name="gla_mlstm_layer",
    )(*operands)
```

```python
import math

import jax
import jax.numpy as jnp
from jax import lax
from jax.experimental import pallas as pl
from jax.experimental.pallas import tpu as pltpu

F32 = jnp.float32
BF16 = jnp.bfloat16

D_MODEL = 1024
GLA_HEADS = 4
GLA_DK = 128
GLA_DV = 256
GLA_KEY_WIDTH = GLA_HEADS * GLA_DK
GLA_WIDTH = GLA_HEADS * GLA_DV
GLA_GATE_RANK = 16
GLA_GATE_TAU = 16.0
MLSTM_HEADS = 4
MLSTM_DH = 256
MLSTM_WIDTH = MLSTM_HEADS * MLSTM_DH
QKV_BLOCK = 4
CONV_WIDTH = 4
EPS = 1e-6

LANES = 128
SUBLANES = 8
TS = 256
TB = 1024
CG = 128
GLA_LEVELS = 7
HIST = SUBLANES
VMEM_LIMIT_BYTES = 60 * 1024 * 1024
LOG2E = math.log2(math.e)


def _dot(a, b):
    return jnp.dot(a, b, preferred_element_type=F32)


def _dot_nt(a, b):
    return lax.dot_general(a, b, (((1,), (1,)), ((), ())), preferred_element_type=F32)


def _dot_tn(a, b):
    return lax.dot_general(a, b, (((0,), (0,)), ((), ())), preferred_element_type=F32)


def _log_sigmoid(z):
    return jnp.minimum(z, 0.0) - jnp.log(1.0 + jnp.exp(-jnp.abs(z)))


def _silu(z):
    half = 0.5 * z
    return half * jnp.tanh(half) + half


def _split3(x):
    h1 = x.astype(BF16)
    r1 = x - h1.astype(F32)
    h2 = r1.astype(BF16)
    h3 = (r1 - h2.astype(F32)).astype(BF16)
    return h1, h2, h3


def _cumsum_rows(tri, x):
    h1, h2, h3 = _split3(x)
    return _dot(tri, h1) + _dot(tri, h2) + _dot(tri, h3)


def _cumsum_rows2(tri, x):
    h1 = x.astype(BF16)
    h2 = (x - h1.astype(F32)).astype(BF16)
    return _dot(tri, h1) + _dot(tri, h2)


def _tri(n):
    row = jnp.arange(n)[:, None]
    col = jnp.arange(n)[None, :]
    return jnp.where(row >= col, 1.0, 0.0).astype(BF16)


def _pair_level(n, levels):
    row = jnp.arange(n, dtype=jnp.int32)[:, None]
    col = jnp.arange(n, dtype=jnp.int32)[None, :]
    x = jnp.bitwise_xor(row, col)
    lvl = jnp.full((n, n), -1, jnp.int32)
    for p in range(levels):
        lvl = jnp.where(jnp.right_shift(x, p) == 1, p, lvl)
    return jnp.where(row > col, lvl, -1)


def _level_exp2_scale(n, levels):
    row = jnp.arange(n, dtype=jnp.int32)[None, :, None]
    p = jnp.arange(levels, dtype=jnp.int32)[:, None, None]
    upper = jnp.bitwise_and(jnp.right_shift(row, p), 1) == 1
    return jnp.broadcast_to(jnp.where(upper, LOG2E, -LOG2E), (levels, n, LANES)).astype(F32)


def _midpoint_rows(b_ref, lanes, p):
    s = 1 << (p + 1)
    half = s // 2
    if s >= SUBLANES:
        pieces = []
        for g in range(CG // s):
            row = g * s + half - 1
            pieces.append(jnp.broadcast_to(b_ref[row:row + 1, lanes], (s, LANES)))
        return pieces[0] if len(pieces) == 1 else jnp.concatenate(pieces, axis=0)
    assert s == 4
    sub = lax.broadcasted_iota(jnp.int32, (SUBLANES, LANES), 0)
    pieces = []
    for g in range(CG // SUBLANES):
        lo = jnp.broadcast_to(b_ref[g * 8 + 1:g * 8 + 2, lanes], (SUBLANES, LANES))
        hi = jnp.broadcast_to(b_ref[g * 8 + 5:g * 8 + 6, lanes], (SUBLANES, LANES))
        pieces.append(jnp.where(sub < 4, lo, hi))
    return jnp.concatenate(pieces, axis=0)


def _layer_kernel(x_ref, ng_ref, wgla_ref, wml_ref, wr_ref, wup_ref, bg_ref, glag_ref, cw_ref, cb_ref,
                  bd_ref, wgate_ref, bgate_ref, mng_ref, skip_ref, wout_ref, fng_ref,
                  trig_ref, lvl_ref, escale_ref, trim_ref,
                  out_ref,
                  u_s, qg_s, kg_s, vg_s, zg_s, la_s, b_s, xm_s, zm_s, cact_s, qm_s, km_s, vm_s,
                  mix_s, sg_s, cm_s, nm_s, mm_s):
    @pl.when(pl.program_id(1) == 0)
    def _reset_state():
        sg_s[...] = jnp.zeros_like(sg_s)
        cm_s[...] = jnp.zeros_like(cm_s)
        nm_s[...] = jnp.zeros_like(nm_s)
        mm_s[...] = jnp.zeros_like(mm_s)
        xm_s[0:HIST, :] = jnp.zeros((HIST, MLSTM_WIDTH), F32)

    def tile(i, carry):
        _process_tile(pl.multiple_of(i * TS, TS),
                      x_ref, ng_ref, wgla_ref, wml_ref, wr_ref, wup_ref, bg_ref, glag_ref, cw_ref, cb_ref,
                      bd_ref, wgate_ref, bgate_ref, mng_ref, skip_ref, wout_ref, fng_ref,
                      trig_ref, lvl_ref, escale_ref, trim_ref, out_ref,
                      u_s, qg_s, kg_s, vg_s, zg_s, la_s, b_s, xm_s, zm_s, cact_s, qm_s, km_s, vm_s,
                      mix_s, sg_s, cm_s, nm_s, mm_s)
        return carry

    lax.fori_loop(0, TB // TS, tile, 0)


def _process_tile(row0, x_ref, ng_ref, wgla_ref, wml_ref, wr_ref, wup_ref, bg_ref, glag_ref, cw_ref, cb_ref,
                  bd_ref, wgate_ref, bgate_ref, mng_ref, skip_ref, wout_ref, fng_ref,
                  trig_ref, lvl_ref, escale_ref, trim_ref, out_ref,
                  u_s, qg_s, kg_s, vg_s, zg_s, la_s, b_s, xm_s, zm_s, cact_s, qm_s, km_s, vm_s,
                  mix_s, sg_s, cm_s, nm_s, mm_s):
    x = x_ref[pl.ds(row0, TS), :]
    ms = jnp.mean(x * x, axis=-1, keepdims=True)
    u_s[...] = (x * lax.rsqrt(ms + EPS) * ng_ref[...]).astype(BF16)
    u = u_s[...]
    r_g = _dot(u, wr_ref[...]).astype(BF16)
    gate = _dot(r_g, wup_ref[...]) + bg_ref[...]
    la_s[...] = _log_sigmoid(gate) * (1.0 / GLA_GATE_TAU)
    c0 = 0
    qg_s[...] = _dot(u, wgla_ref[:, c0:c0 + GLA_KEY_WIDTH]) * (GLA_DK ** -0.5)
    c0 += GLA_KEY_WIDTH
    kg_s[...] = _dot(u, wgla_ref[:, c0:c0 + GLA_KEY_WIDTH])
    c0 += GLA_KEY_WIDTH
    vg_s[...] = _dot(u, wgla_ref[:, c0:c0 + GLA_WIDTH]).astype(BF16)
    c0 += GLA_WIDTH
    zg_s[...] = _dot(u, wgla_ref[:, c0:c0 + GLA_WIDTH])
    xm_s[HIST:HIST + TS, :] = _dot(u, wml_ref[:, 0:MLSTM_WIDTH])
    zm_s[...] = _dot(u, wml_ref[:, MLSTM_WIDTH:2 * MLSTM_WIDTH])

    front = {}

    def front_conv():
        conv = cb_ref[...] + cw_ref[3:4, :] * xm_s[HIST:HIST + TS, :]
        for w in range(CONV_WIDTH - 1):
            off = HIST - (CONV_WIDTH - 1) + w
            conv = conv + cw_ref[w:w + 1, :] * xm_s[off:off + TS, :]
        cact_s[...] = _silu(conv)

    def front_headwise(j):
        cl = slice(j * 256, (j + 1) * 256)
        c_act = cact_s[:, cl].astype(BF16)
        qm_s[:, cl] = _dot(c_act, bd_ref[0, j]).astype(BF16)
        km_s[:, cl] = _dot(c_act, bd_ref[1, j]).astype(BF16)
        vm_s[:, cl] = _dot(xm_s[HIST:HIST + TS, cl].astype(BF16), bd_ref[2, j]).astype(BF16)

    def front_gates():
        xm_s[0:HIST, :] = xm_s[TS:TS + HIST, :]
        front["g"] = (_dot(qm_s[...], wgate_ref[0:MLSTM_WIDTH, :])
                      + _dot(km_s[...], wgate_ref[MLSTM_WIDTH:2 * MLSTM_WIDTH, :])
                      + _dot(vm_s[...], wgate_ref[2 * MLSTM_WIDTH:3 * MLSTM_WIDTH, :])
                      + bgate_ref[...])

    def front_decay():
        g = front["g"]
        bcum = _cumsum_rows(trim_ref[...], _log_sigmoid(g))
        bcum = pltpu.roll(bcum, LANES - MLSTM_HEADS, axis=1)
        front["bcum"] = bcum
        front["a_all"] = g - bcum
        front["a_all_t"] = front["a_all"].T

    front_pieces = [front_conv,
                    lambda: (front_headwise(0), front_headwise(1)),
                    lambda: (front_headwise(2), front_headwise(3)),
                    front_gates, front_decay]

    lvl = lvl_ref[...]
    for c in range(TS // CG):
        rows = slice(c * CG, (c + 1) * CG)
        b_s[...] = _cumsum_rows2(trig_ref[...], la_s[rows, :])
        for h in range(GLA_HEADS):
            kl = slice(h * GLA_DK, (h + 1) * GLA_DK)
            vl = slice(h * GLA_DV, (h + 1) * GLA_DV)
            q = qg_s[rows, kl]
            k = kg_s[rows, kl]
            b = b_s[:, kl]
            q16 = q.astype(BF16)
            k16 = k.astype(BF16)
            e0 = jnp.exp(la_s[rows, kl]).astype(BF16)
            sc = jnp.where(lvl == 0, _dot_nt(q16 * e0, k16), 0.0)
            for p in range(1, GLA_LEVELS):
                e = jnp.exp2((b - _midpoint_rows(b_s, kl, p)) * escale_ref[p]).astype(BF16)
                sc = jnp.where(lvl == p, _dot_nt(q16 * e, k16 * e), sc)
            v = vg_s[rows, vl]
            st = sg_s[h]
            diag = jnp.sum(q * k, axis=-1, keepdims=True)
            o = (_dot(sc.astype(BF16), v) + diag * v.astype(F32)
                 + _dot_nt((q * jnp.exp(b)).astype(BF16), st.astype(BF16)))
            b_last = b[CG - 1:CG, :]
            k_dec = (k * jnp.exp(b_last - b)).astype(BF16)
            sg_s[h] = st * jnp.exp(b_last) + _dot_tn(v, k_dec)
            o = o * lax.rsqrt(jnp.mean(o * o, axis=-1, keepdims=True) + EPS) * glag_ref[:, vl]
            mix_s[rows, vl] = (o * _silu(zg_s[rows, vl])).astype(BF16)
            if front_pieces:
                front_pieces.pop(0)()

    assert not front_pieces
    bcum, a_all, a_all_t = front["bcum"], front["a_all"], front["a_all_t"]
    causal = (lax.broadcasted_iota(jnp.int32, (TS, TS), 0)
              >= lax.broadcasted_iota(jnp.int32, (TS, TS), 1))
    log_kscale = 0.5 * math.log(MLSTM_DH)
    for h in range(MLSTM_HEADS):
        hl = slice(h * MLSTM_DH, (h + 1) * MLSTM_DH)
        a_row = a_all_t[h:h + 1, :]
        a_col = a_all[:, h:h + 1]
        b_col = bcum[:, h:h + 1]
        m_prev = mm_s[h][0:1, 0:1]
        run_max = jnp.max(jnp.where(causal, a_row, -jnp.inf), axis=-1, keepdims=True)
        m_rel = jnp.maximum(m_prev, run_max)
        dmat = jnp.exp(jnp.where(causal, (a_row - log_kscale) - m_rel, -jnp.inf))
        q = qm_s[:, hl]
        k = km_s[:, hl]
        v = vm_s[:, hl]
        s = _dot_nt(q, k) * dmat
        w_inter = jnp.exp(m_prev - m_rel)
        ct = cm_s[h]
        n_row = nm_s[h][0:1, :]
        num = _dot(s.astype(BF16), v) + w_inter * _dot(q, ct.astype(BF16))
        den = (jnp.sum(s, axis=-1, keepdims=True)
               + w_inter * jnp.sum(q.astype(F32) * n_row, axis=-1, keepdims=True))
        m_abs = b_col + m_rel
        hh = num * (1.0 / jnp.maximum(jnp.abs(den), jnp.exp(-m_abs)))
        hh = hh * lax.rsqrt(jnp.mean(hh * hh, axis=-1, keepdims=True) + EPS) * mng_ref[:, hl]
        o = (hh + skip_ref[:, hl] * cact_s[:, hl]) * _silu(zm_s[:, hl])
        mix_s[:, GLA_WIDTH + h * MLSTM_DH:GLA_WIDTH + (h + 1) * MLSTM_DH] = o.astype(BF16)
        m_last = m_rel[TS - 1:TS, :]
        decay = jnp.exp(m_prev - m_last)
        wk_col = jnp.exp((a_col - log_kscale) - m_last)
        wk_row = jnp.exp((a_row - log_kscale) - m_last)
        vw = (v.astype(F32) * wk_col).astype(BF16)
        cm_s[h] = decay * ct + _dot_tn(k, vw)
        n_add = _dot(jnp.broadcast_to(wk_row, (SUBLANES, TS)).astype(BF16), k)
        nm_s[h] = decay * nm_s[h] + n_add
        mm_s[h] = jnp.broadcast_to(b_col[TS - 1:TS, :] + m_last, (SUBLANES, LANES))

    y = x_ref[pl.ds(row0, TS), :] + _dot(mix_s[...], wout_ref[...])
    out_ref[pl.ds(row0, TS), :] = (y * lax.rsqrt(jnp.mean(y * y, axis=-1, keepdims=True) + EPS)
                                   * fng_ref[...])


def _block_diag_tiles(ws):
    wf = jnp.stack(ws).reshape(-1, QKV_BLOCK).astype(BF16)
    sel = (jnp.arange(256)[None, :] % QKV_BLOCK == jnp.arange(QKV_BLOCK)[:, None]).astype(BF16)
    spread = jnp.dot(wf, sel, preferred_element_type=F32)
    r = (jnp.arange(wf.shape[0])[:, None] % 256) // QKV_BLOCK
    c = jnp.arange(256)[None, :] // QKV_BLOCK
    return jnp.where(r == c, spread, 0.0).astype(BF16).reshape(len(ws), -1, 256, 256)


def _prep_w_in_kernel(gla_in_ref, ml0_ref, ml1_ref, tail_ref, gla_ref, ml_ref, r_ref):
    gla_ref[...] = gla_in_ref[...].astype(BF16)
    slab = jnp.concatenate([ml0_ref[...], ml1_ref[...], tail_ref[...]], axis=1)
    ml_ref[...] = slab[:, GLA_GATE_RANK:GLA_GATE_RANK + 2 * MLSTM_WIDTH].astype(BF16)
    lane = lax.broadcasted_iota(jnp.int32, (slab.shape[0], LANES), 1)
    r_ref[...] = jnp.where(lane < GLA_GATE_RANK, slab[:, 0:LANES], 0.0).astype(BF16)


def _prep_w_in(w_in):
    rows = 128
    d = w_in.shape[0]
    r0 = 2 * GLA_KEY_WIDTH + 2 * GLA_WIDTH
    return pl.pallas_call(
        _prep_w_in_kernel,
        out_shape=(jax.ShapeDtypeStruct((d, r0), BF16), jax.ShapeDtypeStruct((d, 2 * MLSTM_WIDTH), BF16),
                   jax.ShapeDtypeStruct((d, LANES), BF16)),
        grid=(d // rows,),
        in_specs=[pl.BlockSpec((rows, r0), lambda i: (i, 0)),
                  pl.BlockSpec((rows, MLSTM_WIDTH), lambda i: (i, r0 // MLSTM_WIDTH)),
                  pl.BlockSpec((rows, MLSTM_WIDTH), lambda i: (i, r0 // MLSTM_WIDTH + 1)),
                  pl.BlockSpec((rows, LANES), lambda i: (i, (r0 + 2 * MLSTM_WIDTH) // LANES))],
        out_specs=(pl.BlockSpec((rows, r0), lambda i: (i, 0)), pl.BlockSpec((rows, 2 * MLSTM_WIDTH), lambda i: (i, 0)),
                   pl.BlockSpec((rows, LANES), lambda i: (i, 0))),
        name="prep_w_in",
    )(w_in, w_in, w_in, w_in)


def kernel(x, norm_g, w_in, w_gla_gate_up, b_gla_gate, gla_norm_g, conv_w, conv_b, w_q_m, w_k_m, w_v_m, w_igate, b_igate, w_fgate, b_fgate, mlstm_norm_g, mlstm_skip, w_out, final_norm_g):
    bsz, seq, d = x.shape
    assert d == D_MODEL and seq % TB == 0 and TB % TS == 0

    r0 = 2 * GLA_KEY_WIDTH + 2 * GLA_WIDTH
    w_gla, w_ml, w_r = _prep_w_in(w_in)
    w_up = jnp.pad(w_gla_gate_up, ((0, LANES - GLA_GATE_RANK), (0, 0))).astype(BF16)
    bd = _block_diag_tiles([w_q_m, w_k_m, w_v_m])
    w_gate = jnp.pad(jnp.concatenate([w_igate, w_fgate], axis=1),
                     ((0, 0), (0, LANES - 2 * MLSTM_HEADS))).astype(BF16)
    b_gate = jnp.pad(jnp.concatenate([b_igate, b_fgate]), (0, LANES - 2 * MLSTM_HEADS)).reshape(1, LANES)
    conv_w8 = jnp.pad(conv_w, ((0, SUBLANES - CONV_WIDTH), (0, 0)))
    row = lambda a: a.reshape(1, -1)

    operands = [
        x, row(norm_g), w_gla, w_ml, w_r, w_up, row(b_gla_gate), row(gla_norm_g), conv_w8, row(conv_b),
        bd, w_gate, b_gate, row(mlstm_norm_g), row(mlstm_skip), w_out.astype(BF16),
        row(final_norm_g),
        _tri(CG), _pair_level(CG, GLA_LEVELS), _level_exp2_scale(CG, GLA_LEVELS), _tri(TS),
    ]

    def const_spec(a):
        nd = a.ndim
        return pl.BlockSpec(a.shape, lambda b, t, _nd=nd: (0,) * _nd, pipeline_mode=pl.Buffered(1))

    tok_spec = pl.BlockSpec((None, TB, D_MODEL), lambda b, t: (b, t, 0))
    in_specs = [tok_spec] + [const_spec(a) for a in operands[1:]]

    scratch = [
        pltpu.VMEM((TS, D_MODEL), BF16),
        pltpu.VMEM((TS, GLA_KEY_WIDTH), F32),
        pltpu.VMEM((TS, GLA_KEY_WIDTH), F32),
        pltpu.VMEM((TS, GLA_WIDTH), BF16),
        pltpu.VMEM((TS, GLA_WIDTH), F32),
        pltpu.VMEM((TS, GLA_KEY_WIDTH), F32),
        pltpu.VMEM((CG, GLA_KEY_WIDTH), F32),
        pltpu.VMEM((HIST + TS, MLSTM_WIDTH), F32),
        pltpu.VMEM((TS, MLSTM_WIDTH), F32),
        pltpu.VMEM((TS, MLSTM_WIDTH), F32),
        pltpu.VMEM((TS, MLSTM_WIDTH), BF16),
        pltpu.VMEM((TS, MLSTM_WIDTH), BF16),
        pltpu.VMEM((TS, MLSTM_WIDTH), BF16),
        pltpu.VMEM((TS, GLA_WIDTH + MLSTM_WIDTH), BF16),
        pltpu.VMEM((GLA_HEADS, GLA_DV, GLA_DK), F32),
        pltpu.VMEM((MLSTM_HEADS, MLSTM_DH, MLSTM_DH), F32),
        pltpu.VMEM((MLSTM_HEADS, SUBLANES, MLSTM_DH), F32),
        pltpu.VMEM((MLSTM_HEADS, SUBLANES, LANES), F32),
    ]

    return pl.pallas_call(
        _layer_kernel,
        out_shape=jax.ShapeDtypeStruct(x.shape, x.dtype),
        grid=(bsz, seq // TB),
        in_specs=in_specs,
        out_specs=tok_spec,
        scratch_shapes=scratch,
        compiler_params=pltpu.CompilerParams(
            dimension_semantics=("arbitrary", "arbitrary"),
            vmem_limit_bytes=VMEM_LIMIT_BYTES),
        name="gla_mlstm_layer",
    )(*operands)
```

```python
import math

import jax
import jax.numpy as jnp
from jax import lax
from jax.experimental import pallas as pl
from jax.experimental.pallas import tpu as pltpu

F32 = jnp.float32
BF16 = jnp.bfloat16

D_MODEL = 1024
GLA_HEADS = 4
GLA_DK = 128
GLA_DV = 256
GLA_KEY_WIDTH = GLA_HEADS * GLA_DK
GLA_WIDTH = GLA_HEADS * GLA_DV
GLA_GATE_RANK = 16
GLA_GATE_TAU = 16.0
MLSTM_HEADS = 4
MLSTM_DH = 256
MLSTM_WIDTH = MLSTM_HEADS * MLSTM_DH
QKV_BLOCK = 4
CONV_WIDTH = 4
EPS = 1e-6

LANES = 128
SUBLANES = 8
TS = 512
TB = 512
CM = 256
CG = 128
GLA_LEVELS = 7
HIST = SUBLANES
VMEM_LIMIT_BYTES = 60 * 1024 * 1024
LOG2E = math.log2(math.e)


def _dot(a, b):
    return jnp.dot(a, b, preferred_element_type=F32)


def _dot_nt(a, b):
    return lax.dot_general(a, b, (((1,), (1,)), ((), ())), preferred_element_type=F32)


def _dot_tn(a, b):
    return lax.dot_general(a, b, (((0,), (0,)), ((), ())), preferred_element_type=F32)


def _log_sigmoid(z):
    return jnp.minimum(z, 0.0) - jnp.log(1.0 + jnp.exp(-jnp.abs(z)))


def _silu(z):
    half = 0.5 * z
    return half * jnp.tanh(half) + half


def _split3(x):
    h1 = x.astype(BF16)
    r1 = x - h1.astype(F32)
    h2 = r1.astype(BF16)
    h3 = (r1 - h2.astype(F32)).astype(BF16)
    return h1, h2, h3


def _cumsum_rows(tri, x):
    h1, h2, h3 = _split3(x)
    return _dot(tri, h1) + _dot(tri, h2) + _dot(tri, h3)


def _cumsum_rows2(tri, x):
    h1 = x.astype(BF16)
    h2 = (x - h1.astype(F32)).astype(BF16)
    return _dot(tri, h1) + _dot(tri, h2)


def _tri(n):
    row = jnp.arange(n)[:, None]
    col = jnp.arange(n)[None, :]
    return jnp.where(row >= col, 1.0, 0.0).astype(BF16)


def _pair_level(n, levels):
    row = jnp.arange(n, dtype=jnp.int32)[:, None]
    col = jnp.arange(n, dtype=jnp.int32)[None, :]
    x = jnp.bitwise_xor(row, col)
    lvl = jnp.full((n, n), -1, jnp.int32)
    for p in range(levels):
        lvl = jnp.where(jnp.right_shift(x, p) == 1, p, lvl)
    return jnp.where(row > col, lvl, -1)


def _level_exp2_scale(n, levels):
    row = jnp.arange(n, dtype=jnp.int32)[None, :, None]
    p = jnp.arange(levels, dtype=jnp.int32)[:, None, None]
    upper = jnp.bitwise_and(jnp.right_shift(row, p), 1) == 1
    return jnp.broadcast_to(jnp.where(upper, LOG2E, -LOG2E), (levels, n, LANES)).astype(F32)


def _midpoint_rows(b_ref, lanes, p):
    s = 1 << (p + 1)
    half = s // 2
    if s >= SUBLANES:
        pieces = []
        for g in range(CG // s):
            row = g * s + half - 1
            pieces.append(jnp.broadcast_to(b_ref[row:row + 1, lanes], (s, LANES)))
        return pieces[0] if len(pieces) == 1 else jnp.concatenate(pieces, axis=0)
    assert s == 4
    sub = lax.broadcasted_iota(jnp.int32, (SUBLANES, LANES), 0)
    pieces = []
    for g in range(CG // SUBLANES):
        lo = jnp.broadcast_to(b_ref[g * 8 + 1:g * 8 + 2, lanes], (SUBLANES, LANES))
        hi = jnp.broadcast_to(b_ref[g * 8 + 5:g * 8 + 6, lanes], (SUBLANES, LANES))
        pieces.append(jnp.where(sub < 4, lo, hi))
    return jnp.concatenate(pieces, axis=0)


def _layer_kernel(x_ref, ng_ref, wgla_ref, wml_ref, wr_ref, wup_ref, bg_ref, glag_ref, cw_ref, cb_ref,
                  bd_ref, wgate_ref, bgate_ref, mng_ref, skip_ref, wout_ref, fng_ref,
                  trig_ref, lvl_ref, escale_ref, trim_ref,
                  out_ref,
                  u_s, qg_s, kg_s, vg_s, zg_s, la_s, b_s, xm_s, zm_s, cact_s, qm_s, km_s, vm_s,
                  mix_s, sg_s, cm_s, nm_s, mm_s):
    @pl.when(pl.program_id(1) == 0)
    def _reset_state():
        sg_s[...] = jnp.zeros_like(sg_s)
        cm_s[...] = jnp.zeros_like(cm_s)
        nm_s[...] = jnp.zeros_like(nm_s)
        mm_s[...] = jnp.zeros_like(mm_s)
        xm_s[0:HIST, :] = jnp.zeros((HIST, MLSTM_WIDTH), F32)

    def tile(i, carry):
        _process_tile(pl.multiple_of(i * TS, TS),
                      x_ref, ng_ref, wgla_ref, wml_ref, wr_ref, wup_ref, bg_ref, glag_ref, cw_ref, cb_ref,
                      bd_ref, wgate_ref, bgate_ref, mng_ref, skip_ref, wout_ref, fng_ref,
                      trig_ref, lvl_ref, escale_ref, trim_ref, out_ref,
                      u_s, qg_s, kg_s, vg_s, zg_s, la_s, b_s, xm_s, zm_s, cact_s, qm_s, km_s, vm_s,
                      mix_s, sg_s, cm_s, nm_s, mm_s)
        return carry

    lax.fori_loop(0, TB // TS, tile, 0)


def _process_tile(row0, x_ref, ng_ref, wgla_ref, wml_ref, wr_ref, wup_ref, bg_ref, glag_ref, cw_ref, cb_ref,
                  bd_ref, wgate_ref, bgate_ref, mng_ref, skip_ref, wout_ref, fng_ref,
                  trig_ref, lvl_ref, escale_ref, trim_ref, out_ref,
                  u_s, qg_s, kg_s, vg_s, zg_s, la_s, b_s, xm_s, zm_s, cact_s, qm_s, km_s, vm_s,
                  mix_s, sg_s, cm_s, nm_s, mm_s):
    x = x_ref[pl.ds(row0, TS), :]
    ms = jnp.mean(x * x, axis=-1, keepdims=True)
    u_s[...] = (x * lax.rsqrt(ms + EPS) * ng_ref[...]).astype(BF16)
    u = u_s[...]
    r_g = _dot(u, wr_ref[...]).astype(BF16)
    gate = _dot(r_g, wup_ref[...]) + bg_ref[...]
    la_s[...] = _log_sigmoid(gate) * (1.0 / GLA_GATE_TAU)
    c0 = 0
    qg_s[...] = _dot(u, wgla_ref[:, c0:c0 + GLA_KEY_WIDTH]) * (GLA_DK ** -0.5)
    c0 += GLA_KEY_WIDTH
    kg_s[...] = _dot(u, wgla_ref[:, c0:c0 + GLA_KEY_WIDTH])
    c0 += GLA_KEY_WIDTH
    vg_s[...] = _dot(u, wgla_ref[:, c0:c0 + GLA_WIDTH]).astype(BF16)
    c0 += GLA_WIDTH
    zg_s[...] = _dot(u, wgla_ref[:, c0:c0 + GLA_WIDTH])
    xm_s[HIST:HIST + TS, :] = _dot(u, wml_ref[:, 0:MLSTM_WIDTH])
    zm_s[...] = _dot(u, wml_ref[:, MLSTM_WIDTH:2 * MLSTM_WIDTH])

    front = {}

    def front_conv():
        conv = cb_ref[...] + cw_ref[3:4, :] * xm_s[HIST:HIST + TS, :]
        for w in range(CONV_WIDTH - 1):
            off = HIST - (CONV_WIDTH - 1) + w
            conv = conv + cw_ref[w:w + 1, :] * xm_s[off:off + TS, :]
        cact_s[...] = _silu(conv)

    def front_headwise(j):
        cl = slice(j * 256, (j + 1) * 256)
        c_act = cact_s[:, cl].astype(BF16)
        qm_s[:, cl] = _dot(c_act, bd_ref[0, j]).astype(BF16)
        km_s[:, cl] = _dot(c_act, bd_ref[1, j]).astype(BF16)
        vm_s[:, cl] = _dot(xm_s[HIST:HIST + TS, cl].astype(BF16), bd_ref[2, j]).astype(BF16)

    def front_gates():
        xm_s[0:HIST, :] = xm_s[TS:TS + HIST, :]
        front["g"] = (_dot(qm_s[...], wgate_ref[0:MLSTM_WIDTH, :])
                      + _dot(km_s[...], wgate_ref[MLSTM_WIDTH:2 * MLSTM_WIDTH, :])
                      + _dot(vm_s[...], wgate_ref[2 * MLSTM_WIDTH:3 * MLSTM_WIDTH, :])
                      + bgate_ref[...])

    def front_decay():
        for mc in range(TS // CM):
            g = front["g"][mc * CM:(mc + 1) * CM, :]
            bcum = _cumsum_rows(trim_ref[...], _log_sigmoid(g))
            bcum = pltpu.roll(bcum, LANES - MLSTM_HEADS, axis=1)
            a_all = g - bcum
            front[mc] = (bcum, a_all, a_all.T)

    front_pieces = [front_conv,
                    lambda: (front_headwise(0), front_headwise(1)),
                    lambda: (front_headwise(2), front_headwise(3)),
                    front_gates, front_decay]

    lvl = lvl_ref[...]
    for c in range(TS // CG):
        rows = slice(c * CG, (c + 1) * CG)
        b_s[...] = _cumsum_rows2(trig_ref[...], la_s[rows, :])
        for h in range(GLA_HEADS):
            kl = slice(h * GLA_DK, (h + 1) * GLA_DK)
            vl = slice(h * GLA_DV, (h + 1) * GLA_DV)
            q = qg_s[rows, kl]
            k = kg_s[rows, kl]
            b = b_s[:, kl]
            q16 = q.astype(BF16)
            k16 = k.astype(BF16)
            e0 = jnp.exp(la_s[rows, kl]).astype(BF16)
            sc = jnp.where(lvl == 0, _dot_nt(q16 * e0, k16), 0.0)
            for p in range(1, GLA_LEVELS):
                e = jnp.exp2((b - _midpoint_rows(b_s, kl, p)) * escale_ref[p]).astype(BF16)
                sc = jnp.where(lvl == p, _dot_nt(q16 * e, k16 * e), sc)
            v = vg_s[rows, vl]
            st = sg_s[h]
            diag = jnp.sum(q * k, axis=-1, keepdims=True)
            o = (_dot(sc.astype(BF16), v) + diag * v.astype(F32)
                 + _dot_nt((q * jnp.exp(b)).astype(BF16), st.astype(BF16)))
            b_last = b[CG - 1:CG, :]
            k_dec = (k * jnp.exp(b_last - b)).astype(BF16)
            sg_s[h] = st * jnp.exp(b_last) + _dot_tn(v, k_dec)
            o = o * lax.rsqrt(jnp.mean(o * o, axis=-1, keepdims=True) + EPS) * glag_ref[:, vl]
            mix_s[rows, vl] = (o * _silu(zg_s[rows, vl])).astype(BF16)
            if front_pieces:
                front_pieces.pop(0)()

    assert not front_pieces
    causal = (lax.broadcasted_iota(jnp.int32, (CM, CM), 0)
              >= lax.broadcasted_iota(jnp.int32, (CM, CM), 1))
    log_kscale = 0.5 * math.log(MLSTM_DH)
    for mc in range(TS // CM):
        mrows = slice(mc * CM, (mc + 1) * CM)
        bcum, a_all, a_all_t = front[mc]
        for h in range(MLSTM_HEADS):
            hl = slice(h * MLSTM_DH, (h + 1) * MLSTM_DH)
            a_row = a_all_t[h:h + 1, :]
            a_col = a_all[:, h:h + 1]
            b_col = bcum[:, h:h + 1]
            m_prev = mm_s[h][0:1, 0:1]
            run_max = jnp.max(jnp.where(causal, a_row, -jnp.inf), axis=-1, keepdims=True)
            m_rel = jnp.maximum(m_prev, run_max)
            dmat = jnp.exp(jnp.where(causal, (a_row - log_kscale) - m_rel, -jnp.inf))
            q = qm_s[mrows, hl]
            k = km_s[mrows, hl]
            v = vm_s[mrows, hl]
            s = _dot_nt(q, k) * dmat
            w_inter = jnp.exp(m_prev - m_rel)
            ct = cm_s[h]
            n_row = nm_s[h][0:1, :]
            num = _dot(s.astype(BF16), v) + w_inter * _dot(q, ct.astype(BF16))
            den = (jnp.sum(s, axis=-1, keepdims=True)
                   + w_inter * jnp.sum(q.astype(F32) * n_row, axis=-1, keepdims=True))
            m_abs = b_col + m_rel
            hh = num * (1.0 / jnp.maximum(jnp.abs(den), jnp.exp(-m_abs)))
            hh = hh * lax.rsqrt(jnp.mean(hh * hh, axis=-1, keepdims=True) + EPS) * mng_ref[:, hl]
            o = (hh + skip_ref[:, hl] * cact_s[mrows, hl]) * _silu(zm_s[mrows, hl])
            mix_s[mrows, GLA_WIDTH + h * MLSTM_DH:GLA_WIDTH + (h + 1) * MLSTM_DH] = o.astype(BF16)
            m_last = m_rel[CM - 1:CM, :]
            decay = jnp.exp(m_prev - m_last)
            wk_col = jnp.exp((a_col - log_kscale) - m_last)
            wk_row = jnp.exp((a_row - log_kscale) - m_last)
            vw = (v.astype(F32) * wk_col).astype(BF16)
            cm_s[h] = decay * ct + _dot_tn(k, vw)
            n_add = _dot(jnp.broadcast_to(wk_row, (SUBLANES, CM)).astype(BF16), k)
            nm_s[h] = decay * nm_s[h] + n_add
            mm_s[h] = jnp.broadcast_to(b_col[CM - 1:CM, :] + m_last, (SUBLANES, LANES))

    y = x_ref[pl.ds(row0, TS), :] + _dot(mix_s[...], wout_ref[...])
    out_ref[pl.ds(row0, TS), :] = (y * lax.rsqrt(jnp.mean(y * y, axis=-1, keepdims=True) + EPS)
                                   * fng_ref[...])


def _block_diag_tiles(ws):
    wf = jnp.stack(ws).reshape(-1, QKV_BLOCK).astype(BF16)
    sel = (jnp.arange(256)[None, :] % QKV_BLOCK == jnp.arange(QKV_BLOCK)[:, None]).astype(BF16)
    spread = jnp.dot(wf, sel, preferred_element_type=F32)
    r = (jnp.arange(wf.shape[0])[:, None] % 256) // QKV_BLOCK
    c = jnp.arange(256)[None, :] // QKV_BLOCK
    return jnp.where(r == c, spread, 0.0).astype(BF16).reshape(len(ws), -1, 256, 256)


def _prep_w_in_kernel(w_ref, gla_ref, ml_ref, r_ref):
    r0 = 2 * GLA_KEY_WIDTH + 2 * GLA_WIDTH
    w = w_ref[...]
    gla_ref[...] = w[:, 0:r0].astype(BF16)
    ml_ref[...] = w[:, r0 + GLA_GATE_RANK:r0 + GLA_GATE_RANK + 2 * MLSTM_WIDTH].astype(BF16)
    lane = lax.broadcasted_iota(jnp.int32, (w.shape[0], LANES), 1)
    r_ref[...] = jnp.where(lane < GLA_GATE_RANK, w[:, r0:r0 + LANES], 0.0).astype(BF16)


def _prep_w_in(w_in):
    rows = 128
    d, width = w_in.shape
    r0 = 2 * GLA_KEY_WIDTH + 2 * GLA_WIDTH
    return pl.pallas_call(
        _prep_w_in_kernel,
        out_shape=(jax.ShapeDtypeStruct((d, r0), BF16), jax.ShapeDtypeStruct((d, 2 * MLSTM_WIDTH), BF16),
                   jax.ShapeDtypeStruct((d, LANES), BF16)),
        grid=(d // rows,),
        in_specs=[pl.BlockSpec((rows, width), lambda i: (i, 0))],
        out_specs=(pl.BlockSpec((rows, r0), lambda i: (i, 0)), pl.BlockSpec((rows, 2 * MLSTM_WIDTH), lambda i: (i, 0)),
                   pl.BlockSpec((rows, LANES), lambda i: (i, 0))),
        name="prep_w_in",
    )(w_in)


def kernel(x, norm_g, w_in, w_gla_gate_up, b_gla_gate, gla_norm_g, conv_w, conv_b, w_q_m, w_k_m, w_v_m, w_igate, b_igate, w_fgate, b_fgate, mlstm_norm_g, mlstm_skip, w_out, final_norm_g):
    bsz, seq, d = x.shape
    assert d == D_MODEL and seq % TB == 0 and TB % TS == 0

    r0 = 2 * GLA_KEY_WIDTH + 2 * GLA_WIDTH
    w_gla, w_ml, w_r = _prep_w_in(w_in)
    w_up = jnp.pad(w_gla_gate_up, ((0, LANES - GLA_GATE_RANK), (0, 0))).astype(BF16)
    bd = _block_diag_tiles([w_q_m, w_k_m, w_v_m])
    w_gate = jnp.pad(jnp.concatenate([w_igate, w_fgate], axis=1),
                     ((0, 0), (0, LANES - 2 * MLSTM_HEADS))).astype(BF16)
    b_gate = jnp.pad(jnp.concatenate([b_igate, b_fgate]), (0, LANES - 2 * MLSTM_HEADS)).reshape(1, LANES)
    conv_w8 = jnp.pad(conv_w, ((0, SUBLANES - CONV_WIDTH), (0, 0)))
    row = lambda a: a.reshape(1, -1)

    operands = [
        x, row(norm_g), w_gla, w_ml, w_r, w_up, row(b_gla_gate), row(gla_norm_g), conv_w8, row(conv_b),
        bd, w_gate, b_gate, row(mlstm_norm_g), row(mlstm_skip), w_out.astype(BF16),
        row(final_norm_g),
        _tri(CG), _pair_level(CG, GLA_LEVELS), _level_exp2_scale(CG, GLA_LEVELS), _tri(CM),
    ]

    def const_spec(a):
        nd = a.ndim
        return pl.BlockSpec(a.shape, lambda b, t, _nd=nd: (0,) * _nd, pipeline_mode=pl.Buffered(1))

    tok_spec = pl.BlockSpec((None, TB, D_MODEL), lambda b, t: (b, t, 0))
    in_specs = [tok_spec] + [const_spec(a) for a in operands[1:]]

    scratch = [
        pltpu.VMEM((TS, D_MODEL), BF16),
        pltpu.VMEM((TS, GLA_KEY_WIDTH), F32),
        pltpu.VMEM((TS, GLA_KEY_WIDTH), F32),
        pltpu.VMEM((TS, GLA_WIDTH), BF16),
        pltpu.VMEM((TS, GLA_WIDTH), F32),
        pltpu.VMEM((TS, GLA_KEY_WIDTH), F32),
        pltpu.VMEM((CG, GLA_KEY_WIDTH), F32),
        pltpu.VMEM((HIST + TS, MLSTM_WIDTH), F32),
        pltpu.VMEM((TS, MLSTM_WIDTH), F32),
        pltpu.VMEM((TS, MLSTM_WIDTH), F32),
        pltpu.VMEM((TS, MLSTM_WIDTH), BF16),
        pltpu.VMEM((TS, MLSTM_WIDTH), BF16),
        pltpu.VMEM((TS, MLSTM_WIDTH), BF16),
        pltpu.VMEM((TS, GLA_WIDTH + MLSTM_WIDTH), BF16),
        pltpu.VMEM((GLA_HEADS, GLA_DV, GLA_DK), F32),
        pltpu.VMEM((MLSTM_HEADS, MLSTM_DH, MLSTM_DH), F32),
        pltpu.VMEM((MLSTM_HEADS, SUBLANES, MLSTM_DH), F32),
        pltpu.VMEM((MLSTM_HEADS, SUBLANES, LANES), F32),
    ]

    return pl.pallas_call(
        _layer_kernel,
        out_shape=jax.ShapeDtypeStruct(x.shape, x.dtype),
        grid=(bsz, seq // TB),
        in_specs=in_specs,
        out_specs=tok_spec,
        scratch_shapes=scratch,
        compiler_params=pltpu.CompilerParams(
            dimension_semantics=("arbitrary", "arbitrary"),
            vmem_limit_bytes=VMEM_LIMIT_BYTES),
        name="gla_mlstm_layer",
    )(*operands)
```

```python
import math

import jax
import jax.numpy as jnp
from jax import lax
from jax.experimental import pallas as pl
from jax.experimental.pallas import tpu as pltpu

F32 = jnp.float32
BF16 = jnp.bfloat16

D_MODEL = 1024
GLA_HEADS = 4
GLA_DK = 128
GLA_DV = 256
GLA_KEY_WIDTH = GLA_HEADS * GLA_DK
GLA_WIDTH = GLA_HEADS * GLA_DV
GLA_GATE_RANK = 16
GLA_GATE_TAU = 16.0
MLSTM_HEADS = 4
MLSTM_DH = 256
MLSTM_WIDTH = MLSTM_HEADS * MLSTM_DH
QKV_BLOCK = 4
CONV_WIDTH = 4
EPS = 1e-6

LANES = 128
SUBLANES = 8
TS = 512
TB = 512
CM = 256
CG = 128
GLA_LEVELS = 7
HIST = SUBLANES
VMEM_LIMIT_BYTES = 60 * 1024 * 1024
LOG2E = math.log2(math.e)


def _dot(a, b):
    return jnp.dot(a, b, preferred_element_type=F32)


def _dot_nt(a, b):
    return lax.dot_general(a, b, (((1,), (1,)), ((), ())), preferred_element_type=F32)


def _dot_tn(a, b):
    return lax.dot_general(a, b, (((0,), (0,)), ((), ())), preferred_element_type=F32)


def _log_sigmoid(z):
    return jnp.minimum(z, 0.0) - jnp.log(1.0 + jnp.exp(-jnp.abs(z)))


def _silu(z):
    half = 0.5 * z
    return half * jnp.tanh(half) + half


def _split3(x):
    h1 = x.astype(BF16)
    r1 = x - h1.astype(F32)
    h2 = r1.astype(BF16)
    h3 = (r1 - h2.astype(F32)).astype(BF16)
    return h1, h2, h3


def _cumsum_rows(tri, x):
    h1, h2, h3 = _split3(x)
    return _dot(tri, h1) + _dot(tri, h2) + _dot(tri, h3)


def _cumsum_rows2(tri, x):
    h1 = x.astype(BF16)
    h2 = (x - h1.astype(F32)).astype(BF16)
    return _dot(tri, h1) + _dot(tri, h2)


def _tri(n):
    row = jnp.arange(n)[:, None]
    col = jnp.arange(n)[None, :]
    return jnp.where(row >= col, 1.0, 0.0).astype(BF16)


def _pair_level(n, levels):
    row = jnp.arange(n, dtype=jnp.int32)[:, None]
    col = jnp.arange(n, dtype=jnp.int32)[None, :]
    x = jnp.bitwise_xor(row, col)
    lvl = jnp.full((n, n), -1, jnp.int32)
    for p in range(levels):
        lvl = jnp.where(jnp.right_shift(x, p) == 1, p, lvl)
    return jnp.where(row > col, lvl, -1)


def _level_exp2_scale(n, levels):
    row = jnp.arange(n, dtype=jnp.int32)[None, :, None]
    p = jnp.arange(levels, dtype=jnp.int32)[:, None, None]
    upper = jnp.bitwise_and(jnp.right_shift(row, p), 1) == 1
    return jnp.broadcast_to(jnp.where(upper, LOG2E, -LOG2E), (levels, n, LANES)).astype(F32)


def _midpoint_rows(b_ref, lanes, p):
    s = 1 << (p + 1)
    half = s // 2
    if s >= SUBLANES:
        pieces = []
        for g in range(CG // s):
            row = g * s + half - 1
            pieces.append(jnp.broadcast_to(b_ref[row:row + 1, lanes], (s, LANES)))
        return pieces[0] if len(pieces) == 1 else jnp.concatenate(pieces, axis=0)
    assert s == 4
    sub = lax.broadcasted_iota(jnp.int32, (SUBLANES, LANES), 0)
    pieces = []
    for g in range(CG // SUBLANES):
        lo = jnp.broadcast_to(b_ref[g * 8 + 1:g * 8 + 2, lanes], (SUBLANES, LANES))
        hi = jnp.broadcast_to(b_ref[g * 8 + 5:g * 8 + 6, lanes], (SUBLANES, LANES))
        pieces.append(jnp.where(sub < 4, lo, hi))
    return jnp.concatenate(pieces, axis=0)


def _layer_kernel(x_ref, ng_ref, w16_ref, wup_ref, bg_ref, glag_ref, cw_ref, cb_ref,
                  bd_ref, wgate_ref, bgate_ref, mng_ref, skip_ref, wout_ref, fng_ref,
                  trig_ref, lvl_ref, escale_ref, trim_ref,
                  out_ref,
                  u_s, qg_s, kg_s, vg_s, zg_s, la_s, b_s, xm_s, zm_s, cact_s, qm_s, km_s, vm_s,
                  mix_s, sg_s, cm_s, nm_s, mm_s, wml_s, wr_s):
    @pl.when(pl.program_id(1) == 0)
    def _reset_state():
        sg_s[...] = jnp.zeros_like(sg_s)
        cm_s[...] = jnp.zeros_like(cm_s)
        nm_s[...] = jnp.zeros_like(nm_s)
        mm_s[...] = jnp.zeros_like(mm_s)
        xm_s[0:HIST, :] = jnp.zeros((HIST, MLSTM_WIDTH), F32)

    @pl.when(jnp.logical_and(pl.program_id(0) == 0, pl.program_id(1) == 0))
    def _align_weight_columns():
        r0 = 2 * GLA_KEY_WIDTH + 2 * GLA_WIDTH
        wml_s[...] = w16_ref[:, r0 + GLA_GATE_RANK:r0 + GLA_GATE_RANK + 2 * MLSTM_WIDTH]
        lane = lax.broadcasted_iota(jnp.int32, (D_MODEL, LANES), 1)
        wr_s[...] = jnp.where(lane < GLA_GATE_RANK, w16_ref[:, r0:r0 + LANES].astype(F32), 0.0).astype(BF16)

    def tile(i, carry):
        _process_tile(pl.multiple_of(i * TS, TS),
                      x_ref, ng_ref, w16_ref, wup_ref, bg_ref, glag_ref, cw_ref, cb_ref,
                      bd_ref, wgate_ref, bgate_ref, mng_ref, skip_ref, wout_ref, fng_ref,
                      trig_ref, lvl_ref, escale_ref, trim_ref, out_ref,
                      u_s, qg_s, kg_s, vg_s, zg_s, la_s, b_s, xm_s, zm_s, cact_s, qm_s, km_s, vm_s,
                      mix_s, sg_s, cm_s, nm_s, mm_s, wml_s, wr_s)
        return carry

    lax.fori_loop(0, TB // TS, tile, 0)


def _process_tile(row0, x_ref, ng_ref, w16_ref, wup_ref, bg_ref, glag_ref, cw_ref, cb_ref,
                  bd_ref, wgate_ref, bgate_ref, mng_ref, skip_ref, wout_ref, fng_ref,
                  trig_ref, lvl_ref, escale_ref, trim_ref, out_ref,
                  u_s, qg_s, kg_s, vg_s, zg_s, la_s, b_s, xm_s, zm_s, cact_s, qm_s, km_s, vm_s,
                  mix_s, sg_s, cm_s, nm_s, mm_s, wml_s, wr_s):
    x = x_ref[pl.ds(row0, TS), :]
    ms = jnp.mean(x * x, axis=-1, keepdims=True)
    u_s[...] = (x * lax.rsqrt(ms + EPS) * ng_ref[...]).astype(BF16)
    u = u_s[...]
    r_g = _dot(u, wr_s[...]).astype(BF16)
    gate = _dot(r_g, wup_ref[...]) + bg_ref[...]
    la_s[...] = _log_sigmoid(gate) * (1.0 / GLA_GATE_TAU)
    c0 = 0
    qg_s[...] = _dot(u, w16_ref[:, c0:c0 + GLA_KEY_WIDTH]) * (GLA_DK ** -0.5)
    c0 += GLA_KEY_WIDTH
    kg_s[...] = _dot(u, w16_ref[:, c0:c0 + GLA_KEY_WIDTH])
    c0 += GLA_KEY_WIDTH
    vg_s[...] = _dot(u, w16_ref[:, c0:c0 + GLA_WIDTH]).astype(BF16)
    c0 += GLA_WIDTH
    zg_s[...] = _dot(u, w16_ref[:, c0:c0 + GLA_WIDTH])
    xm_s[HIST:HIST + TS, :] = _dot(u, wml_s[:, 0:MLSTM_WIDTH])
    zm_s[...] = _dot(u, wml_s[:, MLSTM_WIDTH:2 * MLSTM_WIDTH])

    front = {}

    def front_conv():
        conv = cb_ref[...] + cw_ref[3:4, :] * xm_s[HIST:HIST + TS, :]
        for w in range(CONV_WIDTH - 1):
            off = HIST - (CONV_WIDTH - 1) + w
            conv = conv + cw_ref[w:w + 1, :] * xm_s[off:off + TS, :]
        cact_s[...] = _silu(conv)

    def front_headwise(j):
        cl = slice(j * 256, (j + 1) * 256)
        c_act = cact_s[:, cl].astype(BF16)
        qm_s[:, cl] = _dot(c_act, bd_ref[0, j]).astype(BF16)
        km_s[:, cl] = _dot(c_act, bd_ref[1, j]).astype(BF16)
        vm_s[:, cl] = _dot(xm_s[HIST:HIST + TS, cl].astype(BF16), bd_ref[2, j]).astype(BF16)

    def front_gates():
        xm_s[0:HIST, :] = xm_s[TS:TS + HIST, :]
        front["g"] = (_dot(qm_s[...], wgate_ref[0:MLSTM_WIDTH, :])
                      + _dot(km_s[...], wgate_ref[MLSTM_WIDTH:2 * MLSTM_WIDTH, :])
                      + _dot(vm_s[...], wgate_ref[2 * MLSTM_WIDTH:3 * MLSTM_WIDTH, :])
                      + bgate_ref[...])

    def front_decay():
        for mc in range(TS // CM):
            g = front["g"][mc * CM:(mc + 1) * CM, :]
            bcum = _cumsum_rows(trim_ref[...], _log_sigmoid(g))
            bcum = pltpu.roll(bcum, LANES - MLSTM_HEADS, axis=1)
            a_all = g - bcum
            front[mc] = (bcum, a_all, a_all.T)

    front_pieces = [front_conv,
                    lambda: (front_headwise(0), front_headwise(1)),
                    lambda: (front_headwise(2), front_headwise(3)),
                    front_gates, front_decay]

    lvl = lvl_ref[...]
    for c in range(TS // CG):
        rows = slice(c * CG, (c + 1) * CG)
        b_s[...] = _cumsum_rows2(trig_ref[...], la_s[rows, :])
        for h in range(GLA_HEADS):
            kl = slice(h * GLA_DK, (h + 1) * GLA_DK)
            vl = slice(h * GLA_DV, (h + 1) * GLA_DV)
            q = qg_s[rows, kl]
            k = kg_s[rows, kl]
            b = b_s[:, kl]
            q16 = q.astype(BF16)
            k16 = k.astype(BF16)
            e0 = jnp.exp(la_s[rows, kl]).astype(BF16)
            sc = jnp.where(lvl == 0, _dot_nt(q16 * e0, k16), 0.0)
            for p in range(1, GLA_LEVELS):
                e = jnp.exp2((b - _midpoint_rows(b_s, kl, p)) * escale_ref[p]).astype(BF16)
                sc = jnp.where(lvl == p, _dot_nt(q16 * e, k16 * e), sc)
            v = vg_s[rows, vl]
            st = sg_s[h]
            diag = jnp.sum(q * k, axis=-1, keepdims=True)
            o = (_dot(sc.astype(BF16), v) + diag * v.astype(F32)
                 + _dot_nt((q * jnp.exp(b)).astype(BF16), st.astype(BF16)))
            b_last = b[CG - 1:CG, :]
            k_dec = (k * jnp.exp(b_last - b)).astype(BF16)
            sg_s[h] = st * jnp.exp(b_last) + _dot_tn(v, k_dec)
            o = o * lax.rsqrt(jnp.mean(o * o, axis=-1, keepdims=True) + EPS) * glag_ref[:, vl]
            mix_s[rows, vl] = (o * _silu(zg_s[rows, vl])).astype(BF16)
            if front_pieces:
                front_pieces.pop(0)()

    assert not front_pieces
    causal = (lax.broadcasted_iota(jnp.int32, (CM, CM), 0)
              >= lax.broadcasted_iota(jnp.int32, (CM, CM), 1))
    log_kscale = 0.5 * math.log(MLSTM_DH)
    for mc in range(TS // CM):
        mrows = slice(mc * CM, (mc + 1) * CM)
        bcum, a_all, a_all_t = front[mc]
        for h in range(MLSTM_HEADS):
            hl = slice(h * MLSTM_DH, (h + 1) * MLSTM_DH)
            a_row = a_all_t[h:h + 1, :]
            a_col = a_all[:, h:h + 1]
            b_col = bcum[:, h:h + 1]
            m_prev = mm_s[h][0:1, 0:1]
            run_max = jnp.max(jnp.where(causal, a_row, -jnp.inf), axis=-1, keepdims=True)
            m_rel = jnp.maximum(m_prev, run_max)
            dmat = jnp.exp(jnp.where(causal, (a_row - log_kscale) - m_rel, -jnp.inf))
            q = qm_s[mrows, hl]
            k = km_s[mrows, hl]
            v = vm_s[mrows, hl]
            s = _dot_nt(q, k) * dmat
            w_inter = jnp.exp(m_prev - m_rel)
            ct = cm_s[h]
            n_row = nm_s[h][0:1, :]
            num = _dot(s.astype(BF16), v) + w_inter * _dot(q, ct.astype(BF16))
            den = (jnp.sum(s, axis=-1, keepdims=True)
                   + w_inter * jnp.sum(q.astype(F32) * n_row, axis=-1, keepdims=True))
            m_abs = b_col + m_rel
            hh = num * (1.0 / jnp.maximum(jnp.abs(den), jnp.exp(-m_abs)))
            hh = hh * lax.rsqrt(jnp.mean(hh * hh, axis=-1, keepdims=True) + EPS) * mng_ref[:, hl]
            o = (hh + skip_ref[:, hl] * cact_s[mrows, hl]) * _silu(zm_s[mrows, hl])
            mix_s[mrows, GLA_WIDTH + h * MLSTM_DH:GLA_WIDTH + (h + 1) * MLSTM_DH] = o.astype(BF16)
            m_last = m_rel[CM - 1:CM, :]
            decay = jnp.exp(m_prev - m_last)
            wk_col = jnp.exp((a_col - log_kscale) - m_last)
            wk_row = jnp.exp((a_row - log_kscale) - m_last)
            vw = (v.astype(F32) * wk_col).astype(BF16)
            cm_s[h] = decay * ct + _dot_tn(k, vw)
            n_add = _dot(jnp.broadcast_to(wk_row, (SUBLANES, CM)).astype(BF16), k)
            nm_s[h] = decay * nm_s[h] + n_add
            mm_s[h] = jnp.broadcast_to(b_col[CM - 1:CM, :] + m_last, (SUBLANES, LANES))

    y = x_ref[pl.ds(row0, TS), :] + _dot(mix_s[...], wout_ref[...])
    out_ref[pl.ds(row0, TS), :] = (y * lax.rsqrt(jnp.mean(y * y, axis=-1, keepdims=True) + EPS)
                                   * fng_ref[...])


def _block_diag_tiles(ws):
    wf = jnp.stack(ws).reshape(-1, QKV_BLOCK).astype(BF16)
    sel = (jnp.arange(256)[None, :] % QKV_BLOCK == jnp.arange(QKV_BLOCK)[:, None]).astype(BF16)
    spread = jnp.dot(wf, sel, preferred_element_type=F32)
    r = (jnp.arange(wf.shape[0])[:, None] % 256) // QKV_BLOCK
    c = jnp.arange(256)[None, :] // QKV_BLOCK
    return jnp.where(r == c, spread, 0.0).astype(BF16).reshape(len(ws), -1, 256, 256)


def kernel(x, norm_g, w_in, w_gla_gate_up, b_gla_gate, gla_norm_g, conv_w, conv_b, w_q_m, w_k_m, w_v_m, w_igate, b_igate, w_fgate, b_fgate, mlstm_norm_g, mlstm_skip, w_out, final_norm_g):
    bsz, seq, d = x.shape
    assert d == D_MODEL and seq % TB == 0 and TB % TS == 0

    w_up = jnp.pad(w_gla_gate_up, ((0, LANES - GLA_GATE_RANK), (0, 0))).astype(BF16)
    bd = _block_diag_tiles([w_q_m, w_k_m, w_v_m])
    w_gate = jnp.pad(jnp.concatenate([w_igate, w_fgate], axis=1),
                     ((0, 0), (0, LANES - 2 * MLSTM_HEADS))).astype(BF16)
    b_gate = jnp.pad(jnp.concatenate([b_igate, b_fgate]), (0, LANES - 2 * MLSTM_HEADS)).reshape(1, LANES)
    conv_w8 = jnp.pad(conv_w, ((0, SUBLANES - CONV_WIDTH), (0, 0)))
    row = lambda a: a.reshape(1, -1)

    operands = [
        x, row(norm_g), w_in.astype(BF16), w_up, row(b_gla_gate), row(gla_norm_g), conv_w8, row(conv_b),
        bd, w_gate, b_gate, row(mlstm_norm_g), row(mlstm_skip), w_out.astype(BF16),
        row(final_norm_g),
        _tri(CG), _pair_level(CG, GLA_LEVELS), _level_exp2_scale(CG, GLA_LEVELS), _tri(CM),
    ]

    def const_spec(a):
        nd = a.ndim
        return pl.BlockSpec(a.shape, lambda b, t, _nd=nd: (0,) * _nd, pipeline_mode=pl.Buffered(1))

    tok_spec = pl.BlockSpec((None, TB, D_MODEL), lambda b, t: (b, t, 0))
    in_specs = [tok_spec] + [const_spec(a) for a in operands[1:]]

    scratch = [
        pltpu.VMEM((TS, D_MODEL), BF16),
        pltpu.VMEM((TS, GLA_KEY_WIDTH), F32),
        pltpu.VMEM((TS, GLA_KEY_WIDTH), F32),
        pltpu.VMEM((TS, GLA_WIDTH), BF16),
        pltpu.VMEM((TS, GLA_WIDTH), F32),
        pltpu.VMEM((TS, GLA_KEY_WIDTH), F32),
        pltpu.VMEM((CG, GLA_KEY_WIDTH), F32),
        pltpu.VMEM((HIST + TS, MLSTM_WIDTH), F32),
        pltpu.VMEM((TS, MLSTM_WIDTH), F32),
        pltpu.VMEM((TS, MLSTM_WIDTH), F32),
        pltpu.VMEM((TS, MLSTM_WIDTH), BF16),
        pltpu.VMEM((TS, MLSTM_WIDTH), BF16),
        pltpu.VMEM((TS, MLSTM_WIDTH), BF16),
        pltpu.VMEM((TS, GLA_WIDTH + MLSTM_WIDTH), BF16),
        pltpu.VMEM((GLA_HEADS, GLA_DV, GLA_DK), F32),
        pltpu.VMEM((MLSTM_HEADS, MLSTM_DH, MLSTM_DH), F32),
        pltpu.VMEM((MLSTM_HEADS, SUBLANES, MLSTM_DH), F32),
        pltpu.VMEM((MLSTM_HEADS, SUBLANES, LANES), F32),
        pltpu.VMEM((D_MODEL, 2 * MLSTM_WIDTH), BF16),
        pltpu.VMEM((D_MODEL, LANES), BF16),
    ]

    return pl.pallas_call(
        _layer_kernel,
        out_shape=jax.ShapeDtypeStruct(x.shape, x.dtype),
        grid=(bsz, seq // TB),
        in_specs=in_specs,
        out_specs=tok_spec,
        scratch_shapes=scratch,
        compiler_params=pltpu.CompilerParams(
            dimension_semantics=("arbitrary", "arbitrary"),
            vmem_limit_bytes=VMEM_LIMIT_BYTES),
        name="gla_mlstm_layer",
    )(*operands)
```

```python
import math

import jax
import jax.numpy as jnp
from jax import lax
from jax.experimental import pallas as pl
from jax.experimental.pallas import tpu as pltpu

F32 = jnp.float32
BF16 = jnp.bfloat16

D_MODEL = 1024
GLA_HEADS = 4
GLA_DK = 128
GLA_DV = 256
GLA_KEY_WIDTH = GLA_HEADS * GLA_DK
GLA_WIDTH = GLA_HEADS * GLA_DV
GLA_GATE_RANK = 16
GLA_GATE_TAU = 16.0
MLSTM_HEADS = 4
MLSTM_DH = 256
MLSTM_WIDTH = MLSTM_HEADS * MLSTM_DH
QKV_BLOCK = 4
CONV_WIDTH = 4
EPS = 1e-6

LANES = 128
SUBLANES = 8
TS = 512
TB = 512
CM = 256
CG = 128
GLA_LEVELS = 7
HIST = SUBLANES
VMEM_LIMIT_BYTES = 60 * 1024 * 1024
LOG2E = math.log2(math.e)


def _dot(a, b):
    return jnp.dot(a, b, preferred_element_type=F32)


def _dot_nt(a, b):
    return lax.dot_general(a, b, (((1,), (1,)), ((), ())), preferred_element_type=F32)


def _dot_tn(a, b):
    return lax.dot_general(a, b, (((0,), (0,)), ((), ())), preferred_element_type=F32)


def _log_sigmoid(z):
    return jnp.minimum(z, 0.0) - jnp.log(1.0 + jnp.exp(-jnp.abs(z)))


def _silu(z):
    half = 0.5 * z
    return half * jnp.tanh(half) + half


def _split3(x):
    h1 = x.astype(BF16)
    r1 = x - h1.astype(F32)
    h2 = r1.astype(BF16)
    h3 = (r1 - h2.astype(F32)).astype(BF16)
    return h1, h2, h3


def _cumsum_rows(tri, x):
    h1, h2, h3 = _split3(x)
    return _dot(tri, h1) + _dot(tri, h2) + _dot(tri, h3)


def _cumsum_rows2(tri, x):
    h1 = x.astype(BF16)
    h2 = (x - h1.astype(F32)).astype(BF16)
    return _dot(tri, h1) + _dot(tri, h2)


def _tri(n):
    row = jnp.arange(n)[:, None]
    col = jnp.arange(n)[None, :]
    return jnp.where(row >= col, 1.0, 0.0).astype(BF16)


def _pair_level(n, levels):
    row = jnp.arange(n, dtype=jnp.int32)[:, None]
    col = jnp.arange(n, dtype=jnp.int32)[None, :]
    x = jnp.bitwise_xor(row, col)
    lvl = jnp.full((n, n), -1, jnp.int32)
    for p in range(levels):
        lvl = jnp.where(jnp.right_shift(x, p) == 1, p, lvl)
    return jnp.where(row > col, lvl, -1)


def _level_exp2_scale(n, levels):
    row = jnp.arange(n, dtype=jnp.int32)[None, :, None]
    p = jnp.arange(levels, dtype=jnp.int32)[:, None, None]
    upper = jnp.bitwise_and(jnp.right_shift(row, p), 1) == 1
    return jnp.broadcast_to(jnp.where(upper, LOG2E, -LOG2E), (levels, n, LANES)).astype(F32)


def _midpoint_rows(b_ref, lanes, p):
    s = 1 << (p + 1)
    half = s // 2
    if s >= SUBLANES:
        pieces = []
        for g in range(CG // s):
            row = g * s + half - 1
            pieces.append(jnp.broadcast_to(b_ref[row:row + 1, lanes], (s, LANES)))
        return pieces[0] if len(pieces) == 1 else jnp.concatenate(pieces, axis=0)
    assert s == 4
    sub = lax.broadcasted_iota(jnp.int32, (SUBLANES, LANES), 0)
    pieces = []
    for g in range(CG // SUBLANES):
        lo = jnp.broadcast_to(b_ref[g * 8 + 1:g * 8 + 2, lanes], (SUBLANES, LANES))
        hi = jnp.broadcast_to(b_ref[g * 8 + 5:g * 8 + 6, lanes], (SUBLANES, LANES))
        pieces.append(jnp.where(sub < 4, lo, hi))
    return jnp.concatenate(pieces, axis=0)


def _layer_kernel(x_ref, ng_ref, w16_ref, wup_ref, bg_ref, glag_ref, cw_ref, cb_ref,
                  bd_ref, wgate_ref, bgate_ref, mng_ref, skip_ref, wout_ref, fng_ref,
                  trig_ref, lvl_ref, escale_ref, trim_ref,
                  out_ref,
                  u_s, qg_s, kg_s, vg_s, zg_s, la_s, b_s, xm_s, zm_s, cact_s, qm_s, km_s, vm_s,
                  mix_s, sg_s, cm_s, nm_s, mm_s, wml_s, wr_s):
    @pl.when(pl.program_id(1) == 0)
    def _reset_state():
        sg_s[...] = jnp.zeros_like(sg_s)
        cm_s[...] = jnp.zeros_like(cm_s)
        nm_s[...] = jnp.zeros_like(nm_s)
        mm_s[...] = jnp.zeros_like(mm_s)
        xm_s[0:HIST, :] = jnp.zeros((HIST, MLSTM_WIDTH), F32)

    @pl.when(jnp.logical_and(pl.program_id(0) == 0, pl.program_id(1) == 0))
    def _align_weight_columns():
        r0 = 2 * GLA_KEY_WIDTH + 2 * GLA_WIDTH
        wml_s[...] = w16_ref[:, r0 + GLA_GATE_RANK:r0 + GLA_GATE_RANK + 2 * MLSTM_WIDTH]
        lane = lax.broadcasted_iota(jnp.int32, (D_MODEL, LANES), 1)
        wr_s[...] = jnp.where(lane < GLA_GATE_RANK, w16_ref[:, r0:r0 + LANES].astype(F32), 0.0).astype(BF16)

    def tile(i, carry):
        _process_tile(pl.multiple_of(i * TS, TS),
                      x_ref, ng_ref, w16_ref, wup_ref, bg_ref, glag_ref, cw_ref, cb_ref,
                      bd_ref, wgate_ref, bgate_ref, mng_ref, skip_ref, wout_ref, fng_ref,
                      trig_ref, lvl_ref, escale_ref, trim_ref, out_ref,
                      u_s, qg_s, kg_s, vg_s, zg_s, la_s, b_s, xm_s, zm_s, cact_s, qm_s, km_s, vm_s,
                      mix_s, sg_s, cm_s, nm_s, mm_s, wml_s, wr_s)
        return carry

    lax.fori_loop(0, TB // TS, tile, 0)


def _process_tile(row0, x_ref, ng_ref, w16_ref, wup_ref, bg_ref, glag_ref, cw_ref, cb_ref,
                  bd_ref, wgate_ref, bgate_ref, mng_ref, skip_ref, wout_ref, fng_ref,
                  trig_ref, lvl_ref, escale_ref, trim_ref, out_ref,
                  u_s, qg_s, kg_s, vg_s, zg_s, la_s, b_s, xm_s, zm_s, cact_s, qm_s, km_s, vm_s,
                  mix_s, sg_s, cm_s, nm_s, mm_s, wml_s, wr_s):
    x = x_ref[pl.ds(row0, TS), :]
    u_s[...] = (x * ng_ref[...]).astype(BF16)
    rs = lax.rsqrt(jnp.mean(x * x, axis=-1, keepdims=True) + EPS)
    u = u_s[...]
    r_g = (_dot(u, wr_s[...]) * rs).astype(BF16)
    gate = _dot(r_g, wup_ref[...]) + bg_ref[...]
    la_s[...] = _log_sigmoid(gate) * (1.0 / GLA_GATE_TAU)
    c0 = 0
    qg_s[...] = _dot(u, w16_ref[:, c0:c0 + GLA_KEY_WIDTH]) * (rs * (GLA_DK ** -0.5))
    c0 += GLA_KEY_WIDTH
    kg_s[...] = _dot(u, w16_ref[:, c0:c0 + GLA_KEY_WIDTH]) * rs
    c0 += GLA_KEY_WIDTH
    vg_s[...] = (_dot(u, w16_ref[:, c0:c0 + GLA_WIDTH]) * rs).astype(BF16)
    c0 += GLA_WIDTH
    zg_s[...] = _dot(u, w16_ref[:, c0:c0 + GLA_WIDTH]) * rs
    xm_s[HIST:HIST + TS, :] = _dot(u, wml_s[:, 0:MLSTM_WIDTH]) * rs
    zm_s[...] = _dot(u, wml_s[:, MLSTM_WIDTH:2 * MLSTM_WIDTH]) * rs

    front = {}

    def front_conv():
        conv = cb_ref[...] + cw_ref[3:4, :] * xm_s[HIST:HIST + TS, :]
        for w in range(CONV_WIDTH - 1):
            off = HIST - (CONV_WIDTH - 1) + w
            conv = conv + cw_ref[w:w + 1, :] * xm_s[off:off + TS, :]
        cact_s[...] = _silu(conv)

    def front_headwise(j):
        cl = slice(j * 256, (j + 1) * 256)
        c_act = cact_s[:, cl].astype(BF16)
        qm_s[:, cl] = _dot(c_act, bd_ref[0, j]).astype(BF16)
        km_s[:, cl] = _dot(c_act, bd_ref[1, j]).astype(BF16)
        vm_s[:, cl] = _dot(xm_s[HIST:HIST + TS, cl].astype(BF16), bd_ref[2, j]).astype(BF16)

    def front_gates():
        xm_s[0:HIST, :] = xm_s[TS:TS + HIST, :]
        front["g"] = (_dot(qm_s[...], wgate_ref[0:MLSTM_WIDTH, :])
                      + _dot(km_s[...], wgate_ref[MLSTM_WIDTH:2 * MLSTM_WIDTH, :])
                      + _dot(vm_s[...], wgate_ref[2 * MLSTM_WIDTH:3 * MLSTM_WIDTH, :])
                      + bgate_ref[...])

    def front_decay():
        for mc in range(TS // CM):
            g = front["g"][mc * CM:(mc + 1) * CM, :]
            bcum = _cumsum_rows(trim_ref[...], _log_sigmoid(g))
            bcum = pltpu.roll(bcum, LANES - MLSTM_HEADS, axis=1)
            a_all = g - bcum
            front[mc] = (bcum, a_all, a_all.T)

    front_pieces = [front_conv,
                    lambda: (front_headwise(0), front_headwise(1)),
                    lambda: (front_headwise(2), front_headwise(3)),
                    front_gates, front_decay]

    lvl = lvl_ref[...]
    for c in range(TS // CG):
        rows = slice(c * CG, (c + 1) * CG)
        b_s[...] = _cumsum_rows2(trig_ref[...], la_s[rows, :])
        for h in range(GLA_HEADS):
            kl = slice(h * GLA_DK, (h + 1) * GLA_DK)
            vl = slice(h * GLA_DV, (h + 1) * GLA_DV)
            q = qg_s[rows, kl]
            k = kg_s[rows, kl]
            b = b_s[:, kl]
            q16 = q.astype(BF16)
            k16 = k.astype(BF16)
            e0 = jnp.exp(la_s[rows, kl]).astype(BF16)
            sc = jnp.where(lvl == 0, _dot_nt(q16 * e0, k16), 0.0)
            for p in range(1, GLA_LEVELS):
                e = jnp.exp2((b - _midpoint_rows(b_s, kl, p)) * escale_ref[p]).astype(BF16)
                sc = jnp.where(lvl == p, _dot_nt(q16 * e, k16 * e), sc)
            v = vg_s[rows, vl]
            st = sg_s[h]
            diag = jnp.sum(q * k, axis=-1, keepdims=True)
            o = (_dot(sc.astype(BF16), v) + diag * v.astype(F32)
                 + _dot_nt((q * jnp.exp(b)).astype(BF16), st.astype(BF16)))
            b_last = b[CG - 1:CG, :]
            k_dec = (k * jnp.exp(b_last - b)).astype(BF16)
            sg_s[h] = st * jnp.exp(b_last) + _dot_tn(v, k_dec)
            o = o * lax.rsqrt(jnp.mean(o * o, axis=-1, keepdims=True) + EPS) * glag_ref[:, vl]
            mix_s[rows, vl] = (o * _silu(zg_s[rows, vl])).astype(BF16)
            if front_pieces:
                front_pieces.pop(0)()

    assert not front_pieces
    causal = (lax.broadcasted_iota(jnp.int32, (CM, CM), 0)
              >= lax.broadcasted_iota(jnp.int32, (CM, CM), 1))
    log_kscale = 0.5 * math.log(MLSTM_DH)
    for mc in range(TS // CM):
        mrows = slice(mc * CM, (mc + 1) * CM)
        bcum, a_all, a_all_t = front[mc]
        for h in range(MLSTM_HEADS):
            hl = slice(h * MLSTM_DH, (h + 1) * MLSTM_DH)
            a_row = a_all_t[h:h + 1, :]
            a_col = a_all[:, h:h + 1]
            b_col = bcum[:, h:h + 1]
            m_prev = mm_s[h][0:1, 0:1]
            run_max = jnp.max(jnp.where(causal, a_row, -jnp.inf), axis=-1, keepdims=True)
            m_rel = jnp.maximum(m_prev, run_max)
            dmat = jnp.exp(jnp.where(causal, (a_row - log_kscale) - m_rel, -jnp.inf))
            q = qm_s[mrows, hl]
            k = km_s[mrows, hl]
            v = vm_s[mrows, hl]
            s = _dot_nt(q, k) * dmat
            w_inter = jnp.exp(m_prev - m_rel)
            ct = cm_s[h]
            n_row = nm_s[h][0:1, :]
            num = _dot(s.astype(BF16), v) + w_inter * _dot(q, ct.astype(BF16))
            den = (jnp.sum(s, axis=-1, keepdims=True)
                   + w_inter * jnp.sum(q.astype(F32) * n_row, axis=-1, keepdims=True))
            m_abs = b_col + m_rel
            hh = num * (1.0 / jnp.maximum(jnp.abs(den), jnp.exp(-m_abs)))
            hh = hh * lax.rsqrt(jnp.mean(hh * hh, axis=-1, keepdims=True) + EPS) * mng_ref[:, hl]
            o = (hh + skip_ref[:, hl] * cact_s[mrows, hl]) * _silu(zm_s[mrows, hl])
            mix_s[mrows, GLA_WIDTH + h * MLSTM_DH:GLA_WIDTH + (h + 1) * MLSTM_DH] = o.astype(BF16)
            m_last = m_rel[CM - 1:CM, :]
            decay = jnp.exp(m_prev - m_last)
            wk_col = jnp.exp((a_col - log_kscale) - m_last)
            wk_row = jnp.exp((a_row - log_kscale) - m_last)
            vw = (v.astype(F32) * wk_col).astype(BF16)
            cm_s[h] = decay * ct + _dot_tn(k, vw)
            n_add = _dot(jnp.broadcast_to(wk_row, (SUBLANES, CM)).astype(BF16), k)
            nm_s[h] = decay * nm_s[h] + n_add
            mm_s[h] = jnp.broadcast_to(b_col[CM - 1:CM, :] + m_last, (SUBLANES, LANES))

    y = x_ref[pl.ds(row0, TS), :] + _dot(mix_s[...], wout_ref[...])
    out_ref[pl.ds(row0, TS), :] = (y * lax.rsqrt(jnp.mean(y * y, axis=-1, keepdims=True) + EPS)
                                   * fng_ref[...])


def _block_diag_tiles(ws):
    wf = jnp.stack(ws).reshape(-1, QKV_BLOCK).astype(BF16)
    sel = (jnp.arange(256)[None, :] % QKV_BLOCK == jnp.arange(QKV_BLOCK)[:, None]).astype(BF16)
    spread = jnp.dot(wf, sel, preferred_element_type=F32)
    r = (jnp.arange(wf.shape[0])[:, None] % 256) // QKV_BLOCK
    c = jnp.arange(256)[None, :] // QKV_BLOCK
    return jnp.where(r == c, spread, 0.0).astype(BF16).reshape(len(ws), -1, 256, 256)


def kernel(x, norm_g, w_in, w_gla_gate_up, b_gla_gate, gla_norm_g, conv_w, conv_b, w_q_m, w_k_m, w_v_m, w_igate, b_igate, w_fgate, b_fgate, mlstm_norm_g, mlstm_skip, w_out, final_norm_g):
    bsz, seq, d = x.shape
    assert d == D_MODEL and seq % TB == 0 and TB % TS == 0

    w_up = jnp.pad(w_gla_gate_up, ((0, LANES - GLA_GATE_RANK), (0, 0))).astype(BF16)
    bd = _block_diag_tiles([w_q_m, w_k_m, w_v_m])
    w_gate = jnp.pad(jnp.concatenate([w_igate, w_fgate], axis=1),
                     ((0, 0), (0, LANES - 2 * MLSTM_HEADS))).astype(BF16)
    b_gate = jnp.pad(jnp.concatenate([b_igate, b_fgate]), (0, LANES - 2 * MLSTM_HEADS)).reshape(1, LANES)
    conv_w8 = jnp.pad(conv_w, ((0, SUBLANES - CONV_WIDTH), (0, 0)))
    row = lambda a: a.reshape(1, -1)

    operands = [
        x, row(norm_g), w_in.astype(BF16), w_up, row(b_gla_gate), row(gla_norm_g), conv_w8, row(conv_b),
        bd, w_gate, b_gate, row(mlstm_norm_g), row(mlstm_skip), w_out.astype(BF16),
        row(final_norm_g),
        _tri(CG), _pair_level(CG, GLA_LEVELS), _level_exp2_scale(CG, GLA_LEVELS), _tri(CM),
    ]

    def const_spec(a):
        nd = a.ndim
        return pl.BlockSpec(a.shape, lambda b, t, _nd=nd: (0,) * _nd, pipeline_mode=pl.Buffered(1))

    tok_spec = pl.BlockSpec((None, TB, D_MODEL), lambda b, t: (b, t, 0))
    in_specs = [tok_spec] + [const_spec(a) for a in operands[1:]]

    scratch = [
        pltpu.VMEM((TS, D_MODEL), BF16),
        pltpu.VMEM((TS, GLA_KEY_WIDTH), F32),
        pltpu.VMEM((TS, GLA_KEY_WIDTH), F32),
        pltpu.VMEM((TS, GLA_WIDTH), BF16),
        pltpu.VMEM((TS, GLA_WIDTH), F32),
        pltpu.VMEM((TS, GLA_KEY_WIDTH), F32),
        pltpu.VMEM((CG, GLA_KEY_WIDTH), F32),
        pltpu.VMEM((HIST + TS, MLSTM_WIDTH), F32),
        pltpu.VMEM((TS, MLSTM_WIDTH), F32),
        pltpu.VMEM((TS, MLSTM_WIDTH), F32),
        pltpu.VMEM((TS, MLSTM_WIDTH), BF16),
        pltpu.VMEM((TS, MLSTM_WIDTH), BF16),
        pltpu.VMEM((TS, MLSTM_WIDTH), BF16),
        pltpu.VMEM((TS, GLA_WIDTH + MLSTM_WIDTH), BF16),
        pltpu.VMEM((GLA_HEADS, GLA_DV, GLA_DK), F32),
        pltpu.VMEM((MLSTM_HEADS, MLSTM_DH, MLSTM_DH), F32),
        pltpu.VMEM((MLSTM_HEADS, SUBLANES, MLSTM_DH), F32),
        pltpu.VMEM((MLSTM_HEADS, SUBLANES, LANES), F32),
        pltpu.VMEM((D_MODEL, 2 * MLSTM_WIDTH), BF16),
        pltpu.VMEM((D_MODEL, LANES), BF16),
    ]

    return pl.pallas_call(
        _layer_kernel,
        out_shape=jax.ShapeDtypeStruct(x.shape, x.dtype),
        grid=(bsz, seq // TB),
        in_specs=in_specs,
        out_specs=tok_spec,
        scratch_shapes=scratch,
        compiler_params=pltpu.CompilerParams(
            dimension_semantics=("arbitrary", "arbitrary"),
            vmem_limit_bytes=VMEM_LIMIT_BYTES),
        name="gla_mlstm_layer",
    )(*operands)
```

```python
import math

import jax
import jax.numpy as jnp
from jax import lax
from jax.experimental import pallas as pl
from jax.experimental.pallas import tpu as pltpu

F32 = jnp.float32
BF16 = jnp.bfloat16

D_MODEL = 1024
GLA_HEADS = 4
GLA_DK = 128
GLA_DV = 256
GLA_KEY_WIDTH = GLA_HEADS * GLA_DK
GLA_WIDTH = GLA_HEADS * GLA_DV
GLA_GATE_RANK = 16
GLA_GATE_TAU = 16.0
MLSTM_HEADS = 4
MLSTM_DH = 256
MLSTM_WIDTH = MLSTM_HEADS * MLSTM_DH
QKV_BLOCK = 4
CONV_WIDTH = 4
EPS = 1e-6

LANES = 128
SUBLANES = 8
TS = 512
TB = 512
CM = 256
CG = 128
GLA_LEVELS = 7
HIST = SUBLANES
VMEM_LIMIT_BYTES = 60 * 1024 * 1024
LOG2E = math.log2(math.e)


def _dot(a, b):
    return jnp.dot(a, b, preferred_element_type=F32)


def _dot_nt(a, b):
    return lax.dot_general(a, b, (((1,), (1,)), ((), ())), preferred_element_type=F32)


def _dot_tn(a, b):
    return lax.dot_general(a, b, (((0,), (0,)), ((), ())), preferred_element_type=F32)


def _log_sigmoid(z):
    return jnp.minimum(z, 0.0) - jnp.log(1.0 + jnp.exp(-jnp.abs(z)))


def _silu(z):
    half = 0.5 * z
    return half * jnp.tanh(half) + half


def _split3(x):
    h1 = x.astype(BF16)
    r1 = x - h1.astype(F32)
    h2 = r1.astype(BF16)
    h3 = (r1 - h2.astype(F32)).astype(BF16)
    return h1, h2, h3


def _cumsum_rows(tri, x):
    h1, h2, h3 = _split3(x)
    return _dot(tri, h1) + _dot(tri, h2) + _dot(tri, h3)


def _cumsum_rows2(tri, x):
    h1 = x.astype(BF16)
    h2 = (x - h1.astype(F32)).astype(BF16)
    return _dot(tri, h1) + _dot(tri, h2)


def _tri(n):
    row = jnp.arange(n)[:, None]
    col = jnp.arange(n)[None, :]
    return jnp.where(row >= col, 1.0, 0.0).astype(BF16)


def _pair_level(n, levels):
    row = jnp.arange(n, dtype=jnp.int32)[:, None]
    col = jnp.arange(n, dtype=jnp.int32)[None, :]
    x = jnp.bitwise_xor(row, col)
    lvl = jnp.full((n, n), -1, jnp.int32)
    for p in range(levels):
        lvl = jnp.where(jnp.right_shift(x, p) == 1, p, lvl)
    return jnp.where(row > col, lvl, -1)


def _level_exp2_scale(n, levels):
    row = jnp.arange(n, dtype=jnp.int32)[None, :, None]
    p = jnp.arange(levels, dtype=jnp.int32)[:, None, None]
    upper = jnp.bitwise_and(jnp.right_shift(row, p), 1) == 1
    return jnp.broadcast_to(jnp.where(upper, LOG2E, -LOG2E), (levels, n, LANES)).astype(F32)


def _midpoint_rows(b_ref, lanes, p):
    s = 1 << (p + 1)
    half = s // 2
    if s >= SUBLANES:
        pieces = []
        for g in range(CG // s):
            row = g * s + half - 1
            pieces.append(jnp.broadcast_to(b_ref[row:row + 1, lanes], (s, LANES)))
        return pieces[0] if len(pieces) == 1 else jnp.concatenate(pieces, axis=0)
    assert s == 4
    sub = lax.broadcasted_iota(jnp.int32, (SUBLANES, LANES), 0)
    pieces = []
    for g in range(CG // SUBLANES):
        lo = jnp.broadcast_to(b_ref[g * 8 + 1:g * 8 + 2, lanes], (SUBLANES, LANES))
        hi = jnp.broadcast_to(b_ref[g * 8 + 5:g * 8 + 6, lanes], (SUBLANES, LANES))
        pieces.append(jnp.where(sub < 4, lo, hi))
    return jnp.concatenate(pieces, axis=0)


def _layer_kernel(x_ref, ng_ref, w16_ref, wup_ref, bg_ref, glag_ref, cw_ref, cb_ref,
                  bd_ref, wgate_ref, bgate_ref, mng_ref, skip_ref, wout_ref, fng_ref,
                  trig_ref, lvl_ref, escale_ref, trim_ref,
                  out_ref,
                  u_s, qg_s, kg_s, vg_s, zg_s, la_s, b_s, xm_s, zm_s, cact_s, qm_s, km_s, vm_s,
                  mix_s, sg_s, cm_s, nm_s, mm_s, wml_s, wr_s):
    @pl.when(pl.program_id(1) == 0)
    def _reset_state():
        sg_s[...] = jnp.zeros_like(sg_s)
        cm_s[...] = jnp.zeros_like(cm_s)
        nm_s[...] = jnp.zeros_like(nm_s)
        mm_s[...] = jnp.zeros_like(mm_s)
        xm_s[0:HIST, :] = jnp.zeros((HIST, MLSTM_WIDTH), F32)

    @pl.when(jnp.logical_and(pl.program_id(0) == 0, pl.program_id(1) == 0))
    def _align_weight_columns():
        r0 = 2 * GLA_KEY_WIDTH + 2 * GLA_WIDTH
        wml_s[...] = w16_ref[:, r0 + GLA_GATE_RANK:r0 + GLA_GATE_RANK + 2 * MLSTM_WIDTH]
        lane = lax.broadcasted_iota(jnp.int32, (D_MODEL, LANES), 1)
        wr_s[...] = jnp.where(lane < GLA_GATE_RANK, w16_ref[:, r0:r0 + LANES].astype(F32), 0.0).astype(BF16)

    def tile(i, carry):
        _process_tile(pl.multiple_of(i * TS, TS),
                      x_ref, ng_ref, w16_ref, wup_ref, bg_ref, glag_ref, cw_ref, cb_ref,
                      bd_ref, wgate_ref, bgate_ref, mng_ref, skip_ref, wout_ref, fng_ref,
                      trig_ref, lvl_ref, escale_ref, trim_ref, out_ref,
                      u_s, qg_s, kg_s, vg_s, zg_s, la_s, b_s, xm_s, zm_s, cact_s, qm_s, km_s, vm_s,
                      mix_s, sg_s, cm_s, nm_s, mm_s, wml_s, wr_s)
        return carry

    lax.fori_loop(0, TB // TS, tile, 0)


def _process_tile(row0, x_ref, ng_ref, w16_ref, wup_ref, bg_ref, glag_ref, cw_ref, cb_ref,
                  bd_ref, wgate_ref, bgate_ref, mng_ref, skip_ref, wout_ref, fng_ref,
                  trig_ref, lvl_ref, escale_ref, trim_ref, out_ref,
                  u_s, qg_s, kg_s, vg_s, zg_s, la_s, b_s, xm_s, zm_s, cact_s, qm_s, km_s, vm_s,
                  mix_s, sg_s, cm_s, nm_s, mm_s, wml_s, wr_s):
    x = x_ref[pl.ds(row0, TS), :]
    u_s[...] = (x * ng_ref[...]).astype(BF16)
    rs = lax.rsqrt(jnp.mean(x * x, axis=-1, keepdims=True) + EPS)
    u = u_s[...]
    r_g = (_dot(u, wr_s[...]) * rs).astype(BF16)
    gate = _dot(r_g, wup_ref[...]) + bg_ref[...]
    la_s[...] = _log_sigmoid(gate) * (1.0 / GLA_GATE_TAU)
    c0 = 0
    qg_s[...] = _dot(u, w16_ref[:, c0:c0 + GLA_KEY_WIDTH]) * (rs * (GLA_DK ** -0.5))
    c0 += GLA_KEY_WIDTH
    kg_s[...] = _dot(u, w16_ref[:, c0:c0 + GLA_KEY_WIDTH]) * rs
    c0 += GLA_KEY_WIDTH
    vg_s[...] = (_dot(u, w16_ref[:, c0:c0 + GLA_WIDTH]) * rs).astype(BF16)
    c0 += GLA_WIDTH
    zg_s[...] = _dot(u, w16_ref[:, c0:c0 + GLA_WIDTH]) * rs
    xm_s[HIST:HIST + TS, :] = _dot(u, wml_s[:, 0:MLSTM_WIDTH]) * rs
    zm_s[...] = _dot(u, wml_s[:, MLSTM_WIDTH:2 * MLSTM_WIDTH]) * rs

    front = {}

    def front_conv():
        conv = cb_ref[...] + cw_ref[3:4, :] * xm_s[HIST:HIST + TS, :]
        for w in range(CONV_WIDTH - 1):
            off = HIST - (CONV_WIDTH - 1) + w
            conv = conv + cw_ref[w:w + 1, :] * xm_s[off:off + TS, :]
        cact_s[...] = _silu(conv)

    def front_headwise(j):
        cl = slice(j * 256, (j + 1) * 256)
        c_act = cact_s[:, cl].astype(BF16)
        qm_s[:, cl] = _dot(c_act, bd_ref[0, j]).astype(BF16)
        km_s[:, cl] = _dot(c_act, bd_ref[1, j]).astype(BF16)
        vm_s[:, cl] = _dot(xm_s[HIST:HIST + TS, cl].astype(BF16), bd_ref[2, j]).astype(BF16)

    def front_gates():
        xm_s[0:HIST, :] = xm_s[TS:TS + HIST, :]
        front["g"] = (_dot(qm_s[...], wgate_ref[0:MLSTM_WIDTH, :])
                      + _dot(km_s[...], wgate_ref[MLSTM_WIDTH:2 * MLSTM_WIDTH, :])
                      + _dot(vm_s[...], wgate_ref[2 * MLSTM_WIDTH:3 * MLSTM_WIDTH, :])
                      + bgate_ref[...])

    def front_decay():
        for mc in range(TS // CM):
            g = front["g"][mc * CM:(mc + 1) * CM, :]
            bcum = _cumsum_rows(trim_ref[...], _log_sigmoid(g))
            bcum = pltpu.roll(bcum, LANES - MLSTM_HEADS, axis=1)
            a_all = g - bcum
            front[mc] = (bcum, a_all, a_all.T)

    front_pieces = [front_conv,
                    lambda: (front_headwise(0), front_headwise(1)),
                    lambda: (front_headwise(2), front_headwise(3)),
                    front_gates, front_decay]

    lvl = lvl_ref[...]
    for c in range(TS // CG):
        rows = slice(c * CG, (c + 1) * CG)
        b_s[...] = _cumsum_rows2(trig_ref[...], la_s[rows, :])
        for h in range(GLA_HEADS):
            kl = slice(h * GLA_DK, (h + 1) * GLA_DK)
            vl = slice(h * GLA_DV, (h + 1) * GLA_DV)
            q = qg_s[rows, kl]
            k = kg_s[rows, kl]
            b = b_s[:, kl]
            q16 = q.astype(BF16)
            k16 = k.astype(BF16)
            e0 = jnp.exp(la_s[rows, kl]).astype(BF16)
            sc = jnp.where(lvl == 0, _dot_nt(q16 * e0, k16), 0.0)
            for p in range(1, GLA_LEVELS):
                e = jnp.exp2((b - _midpoint_rows(b_s, kl, p)) * escale_ref[p]).astype(BF16)
                sc = jnp.where(lvl == p, _dot_nt(q16 * e, k16 * e), sc)
            v = vg_s[rows, vl]
            st = sg_s[h]
            diag = jnp.sum(q * k, axis=-1, keepdims=True)
            o = (_dot(sc.astype(BF16), v) + diag * v.astype(F32)
                 + _dot_nt((q * jnp.exp(b)).astype(BF16), st.astype(BF16)))
            b_last = b[CG - 1:CG, :]
            k_dec = (k * jnp.exp(b_last - b)).astype(BF16)
            sg_s[h] = st * jnp.exp(b_last) + _dot_tn(v, k_dec)
            o = o * lax.rsqrt(jnp.mean(o * o, axis=-1, keepdims=True) + EPS) * glag_ref[:, vl]
            mix_s[rows, vl] = (o * _silu(zg_s[rows, vl])).astype(BF16)
            if front_pieces:
                front_pieces.pop(0)()

    assert not front_pieces
    causal = (lax.broadcasted_iota(jnp.int32, (CM, CM), 0)
              >= lax.broadcasted_iota(jnp.int32, (CM, CM), 1))
    log_kscale = 0.5 * math.log(MLSTM_DH)
    for mc in range(TS // CM):
        mrows = slice(mc * CM, (mc + 1) * CM)
        bcum, a_all, a_all_t = front[mc]
        for h in range(MLSTM_HEADS):
            hl = slice(h * MLSTM_DH, (h + 1) * MLSTM_DH)
            a_row = a_all_t[h:h + 1, :]
            a_col = a_all[:, h:h + 1]
            b_col = bcum[:, h:h + 1]
            m_prev = mm_s[h][0:1, 0:1]
            run_max = jnp.max(jnp.where(causal, a_row, -jnp.inf), axis=-1, keepdims=True)
            m_rel = jnp.maximum(m_prev, run_max)
            dmat = jnp.exp(jnp.where(causal, (a_row - log_kscale) - m_rel, -jnp.inf))
            q = qm_s[mrows, hl]
            k = km_s[mrows, hl]
            v = vm_s[mrows, hl]
            s = _dot_nt(q, k) * dmat
            w_inter = jnp.exp(m_prev - m_rel)
            ct = cm_s[h]
            n_row = nm_s[h][0:1, :]
            num = _dot(s.astype(BF16), v) + w_inter * _dot(q, ct.astype(BF16))
            den = (jnp.sum(s, axis=-1, keepdims=True)
                   + w_inter * jnp.sum(q.astype(F32) * n_row, axis=-1, keepdims=True))
            m_abs = b_col + m_rel
            hh = num * (1.0 / jnp.maximum(jnp.abs(den), jnp.exp(-m_abs)))
            hh = hh * lax.rsqrt(jnp.mean(hh * hh, axis=-1, keepdims=True) + EPS) * mng_ref[:, hl]
            o = (hh + skip_ref[:, hl] * cact_s[mrows, hl]) * _silu(zm_s[mrows, hl])
            mix_s[mrows, GLA_WIDTH + h * MLSTM_DH:GLA_WIDTH + (h + 1) * MLSTM_DH] = o.astype(BF16)
            m_last = m_rel[CM - 1:CM, :]
            decay = jnp.exp(m_prev - m_last)
            wk_col = jnp.exp((a_col - log_kscale) - m_last)
            wk_row = jnp.exp((a_row - log_kscale) - m_last)
            vw = (v.astype(F32) * wk_col).astype(BF16)
            cm_s[h] = decay * ct + _dot_tn(k, vw)
            n_add = _dot(jnp.broadcast_to(wk_row, (SUBLANES, CM)).astype(BF16), k)
            nm_s[h] = decay * nm_s[h] + n_add
            mm_s[h] = jnp.broadcast_to(b_col[CM - 1:CM, :] + m_last, (SUBLANES, LANES))

    for r0 in range(0, TS, CM):
        y = x_ref[pl.ds(row0 + r0, CM), :] + _dot(mix_s[r0:r0 + CM, :], wout_ref[...])
        out_ref[pl.ds(row0 + r0, CM), :] = (y * lax.rsqrt(jnp.mean(y * y, axis=-1, keepdims=True) + EPS)
                                            * fng_ref[...])


def _block_diag_tiles(ws):
    wf = jnp.stack(ws).reshape(-1, QKV_BLOCK).astype(BF16)
    sel = (jnp.arange(256)[None, :] % QKV_BLOCK == jnp.arange(QKV_BLOCK)[:, None]).astype(BF16)
    spread = jnp.dot(wf, sel, preferred_element_type=F32)
    r = (jnp.arange(wf.shape[0])[:, None] % 256) // QKV_BLOCK
    c = jnp.arange(256)[None, :] // QKV_BLOCK
    return jnp.where(r == c, spread, 0.0).astype(BF16).reshape(len(ws), -1, 256, 256)


def kernel(x, norm_g, w_in, w_gla_gate_up, b_gla_gate, gla_norm_g, conv_w, conv_b, w_q_m, w_k_m, w_v_m, w_igate, b_igate, w_fgate, b_fgate, mlstm_norm_g, mlstm_skip, w_out, final_norm_g):
    bsz, seq, d = x.shape
    assert d == D_MODEL and seq % TB == 0 and TB % TS == 0

    w_up = jnp.pad(w_gla_gate_up, ((0, LANES - GLA_GATE_RANK), (0, 0))).astype(BF16)
    bd = _block_diag_tiles([w_q_m, w_k_m, w_v_m])
    w_gate = jnp.pad(jnp.concatenate([w_igate, w_fgate], axis=1),
                     ((0, 0), (0, LANES - 2 * MLSTM_HEADS))).astype(BF16)
    b_gate = jnp.pad(jnp.concatenate([b_igate, b_fgate]), (0, LANES - 2 * MLSTM_HEADS)).reshape(1, LANES)
    conv_w8 = jnp.pad(conv_w, ((0, SUBLANES - CONV_WIDTH), (0, 0)))
    row = lambda a: a.reshape(1, -1)

    operands = [
        x, row(norm_g), w_in.astype(BF16), w_up, row(b_gla_gate), row(gla_norm_g), conv_w8, row(conv_b),
        bd, w_gate, b_gate, row(mlstm_norm_g), row(mlstm_skip), w_out.astype(BF16),
        row(final_norm_g),
        _tri(CG), _pair_level(CG, GLA_LEVELS), _level_exp2_scale(CG, GLA_LEVELS), _tri(CM),
    ]

    def const_spec(a):
        nd = a.ndim
        return pl.BlockSpec(a.shape, lambda b, t, _nd=nd: (0,) * _nd, pipeline_mode=pl.Buffered(1))

    tok_spec = pl.BlockSpec((None, TB, D_MODEL), lambda b, t: (b, t, 0))
    in_specs = [tok_spec] + [const_spec(a) for a in operands[1:]]

    scratch = [
        pltpu.VMEM((TS, D_MODEL), BF16),
        pltpu.VMEM((TS, GLA_KEY_WIDTH), F32),
        pltpu.VMEM((TS, GLA_KEY_WIDTH), F32),
        pltpu.VMEM((TS, GLA_WIDTH), BF16),
        pltpu.VMEM((TS, GLA_WIDTH), F32),
        pltpu.VMEM((TS, GLA_KEY_WIDTH), F32),
        pltpu.VMEM((CG, GLA_KEY_WIDTH), F32),
        pltpu.VMEM((HIST + TS, MLSTM_WIDTH), F32),
        pltpu.VMEM((TS, MLSTM_WIDTH), F32),
        pltpu.VMEM((TS, MLSTM_WIDTH), F32),
        pltpu.VMEM((TS, MLSTM_WIDTH), BF16),
        pltpu.VMEM((TS, MLSTM_WIDTH), BF16),
        pltpu.VMEM((TS, MLSTM_WIDTH), BF16),
        pltpu.VMEM((TS, GLA_WIDTH + MLSTM_WIDTH), BF16),
        pltpu.VMEM((GLA_HEADS, GLA_DV, GLA_DK), F32),
        pltpu.VMEM((MLSTM_HEADS, MLSTM_DH, MLSTM_DH), F32),
        pltpu.VMEM((MLSTM_HEADS, SUBLANES, MLSTM_DH), F32),
        pltpu.VMEM((MLSTM_HEADS, SUBLANES, LANES), F32),
        pltpu.VMEM((D_MODEL, 2 * MLSTM_WIDTH), BF16),
        pltpu.VMEM((D_MODEL, LANES), BF16),
    ]

    return pl.pallas_call(
        _layer_kernel,
        out_shape=jax.ShapeDtypeStruct(x.shape, x.dtype),
        grid=(bsz, seq // TB),
        in_specs=in_specs,
        out_specs=tok_spec,
        scratch_shapes=scratch,
        compiler_params=pltpu.CompilerParams(
            dimension_semantics=("arbitrary", "arbitrary"),
            vmem_limit_bytes=VMEM_LIMIT_BYTES),
        name="gla_mlstm_layer",
    )(*operands)
```

```python
import math

import jax
import jax.numpy as jnp
from jax import lax
from jax.experimental import pallas as pl
from jax.experimental.pallas import tpu as pltpu

F32 = jnp.float32
BF16 = jnp.bfloat16

D_MODEL = 1024
GLA_HEADS = 4
GLA_DK = 128
GLA_DV = 256
GLA_KEY_WIDTH = GLA_HEADS * GLA_DK
GLA_WIDTH = GLA_HEADS * GLA_DV
GLA_GATE_RANK = 16
GLA_GATE_TAU = 16.0
MLSTM_HEADS = 4
MLSTM_DH = 256
MLSTM_WIDTH = MLSTM_HEADS * MLSTM_DH
QKV_BLOCK = 4
CONV_WIDTH = 4
EPS = 1e-6

LANES = 128
SUBLANES = 8
TS = 512
TB = 512
CM = 256
CG = 128
GLA_LEVELS = 7
HIST = SUBLANES
VMEM_LIMIT_BYTES = 60 * 1024 * 1024
LOG2E = math.log2(math.e)


def _dot(a, b):
    return jnp.dot(a, b, preferred_element_type=F32)


def _dot_nt(a, b):
    return lax.dot_general(a, b, (((1,), (1,)), ((), ())), preferred_element_type=F32)


def _dot_tn(a, b):
    return lax.dot_general(a, b, (((0,), (0,)), ((), ())), preferred_element_type=F32)


def _log_sigmoid(z):
    return jnp.minimum(z, 0.0) - jnp.log(1.0 + jnp.exp(-jnp.abs(z)))


def _silu(z):
    half = 0.5 * z
    return half * jnp.tanh(half) + half


def _split3(x):
    h1 = x.astype(BF16)
    r1 = x - h1.astype(F32)
    h2 = r1.astype(BF16)
    h3 = (r1 - h2.astype(F32)).astype(BF16)
    return h1, h2, h3


def _cumsum_rows(tri, x):
    h1, h2, h3 = _split3(x)
    return _dot(tri, h1) + _dot(tri, h2) + _dot(tri, h3)


def _cumsum_rows2(tri, x):
    h1 = x.astype(BF16)
    h2 = (x - h1.astype(F32)).astype(BF16)
    return _dot(tri, h1) + _dot(tri, h2)


def _tri(n):
    row = jnp.arange(n)[:, None]
    col = jnp.arange(n)[None, :]
    return jnp.where(row >= col, 1.0, 0.0).astype(BF16)


def _pair_level(n, levels):
    row = jnp.arange(n, dtype=jnp.int32)[:, None]
    col = jnp.arange(n, dtype=jnp.int32)[None, :]
    x = jnp.bitwise_xor(row, col)
    lvl = jnp.full((n, n), -1, jnp.int32)
    for p in range(levels):
        lvl = jnp.where(jnp.right_shift(x, p) == 1, p, lvl)
    return jnp.where(row > col, lvl, jnp.where(row == col, levels, -1))


def _level_exp2_scale(n, levels):
    row = jnp.arange(n, dtype=jnp.int32)[None, :, None]
    p = jnp.arange(levels, dtype=jnp.int32)[:, None, None]
    upper = jnp.bitwise_and(jnp.right_shift(row, p), 1) == 1
    return jnp.broadcast_to(jnp.where(upper, LOG2E, -LOG2E), (levels, n, LANES)).astype(F32)


def _midpoint_rows(b_ref, lanes, p):
    s = 1 << (p + 1)
    half = s // 2
    if s >= SUBLANES:
        pieces = []
        for g in range(CG // s):
            row = g * s + half - 1
            pieces.append(jnp.broadcast_to(b_ref[row:row + 1, lanes], (s, LANES)))
        return pieces[0] if len(pieces) == 1 else jnp.concatenate(pieces, axis=0)
    assert s == 4
    sub = lax.broadcasted_iota(jnp.int32, (SUBLANES, LANES), 0)
    pieces = []
    for g in range(CG // SUBLANES):
        lo = jnp.broadcast_to(b_ref[g * 8 + 1:g * 8 + 2, lanes], (SUBLANES, LANES))
        hi = jnp.broadcast_to(b_ref[g * 8 + 5:g * 8 + 6, lanes], (SUBLANES, LANES))
        pieces.append(jnp.where(sub < 4, lo, hi))
    return jnp.concatenate(pieces, axis=0)


def _layer_kernel(x_ref, ng_ref, w16_ref, wup_ref, bg_ref, glag_ref, cw_ref, cb_ref,
                  bd_ref, wgate_ref, bgate_ref, mng_ref, skip_ref, wout_ref, fng_ref,
                  trig_ref, lvl_ref, escale_ref, trim_ref,
                  out_ref,
                  u_s, qg_s, kg_s, vg_s, zg_s, la_s, b_s, xm_s, zm_s, cact_s, qm_s, km_s, vm_s,
                  mix_s, sg_s, cm_s, nm_s, mm_s, wml_s, wr_s):
    @pl.when(pl.program_id(1) == 0)
    def _reset_state():
        sg_s[...] = jnp.zeros_like(sg_s)
        cm_s[...] = jnp.zeros_like(cm_s)
        nm_s[...] = jnp.zeros_like(nm_s)
        mm_s[...] = jnp.zeros_like(mm_s)
        xm_s[0:HIST, :] = jnp.zeros((HIST, MLSTM_WIDTH), F32)

    @pl.when(jnp.logical_and(pl.program_id(0) == 0, pl.program_id(1) == 0))
    def _align_weight_columns():
        r0 = 2 * GLA_KEY_WIDTH + 2 * GLA_WIDTH
        wml_s[...] = w16_ref[:, r0 + GLA_GATE_RANK:r0 + GLA_GATE_RANK + 2 * MLSTM_WIDTH]
        lane = lax.broadcasted_iota(jnp.int32, (D_MODEL, LANES), 1)
        wr_s[...] = jnp.where(lane < GLA_GATE_RANK, w16_ref[:, r0:r0 + LANES].astype(F32), 0.0).astype(BF16)

    def tile(i, carry):
        _process_tile(pl.multiple_of(i * TS, TS),
                      x_ref, ng_ref, w16_ref, wup_ref, bg_ref, glag_ref, cw_ref, cb_ref,
                      bd_ref, wgate_ref, bgate_ref, mng_ref, skip_ref, wout_ref, fng_ref,
                      trig_ref, lvl_ref, escale_ref, trim_ref, out_ref,
                      u_s, qg_s, kg_s, vg_s, zg_s, la_s, b_s, xm_s, zm_s, cact_s, qm_s, km_s, vm_s,
                      mix_s, sg_s, cm_s, nm_s, mm_s, wml_s, wr_s)
        return carry

    lax.fori_loop(0, TB // TS, tile, 0)


def _process_tile(row0, x_ref, ng_ref, w16_ref, wup_ref, bg_ref, glag_ref, cw_ref, cb_ref,
                  bd_ref, wgate_ref, bgate_ref, mng_ref, skip_ref, wout_ref, fng_ref,
                  trig_ref, lvl_ref, escale_ref, trim_ref, out_ref,
                  u_s, qg_s, kg_s, vg_s, zg_s, la_s, b_s, xm_s, zm_s, cact_s, qm_s, km_s, vm_s,
                  mix_s, sg_s, cm_s, nm_s, mm_s, wml_s, wr_s):
    x = x_ref[pl.ds(row0, TS), :]
    u_s[...] = (x * ng_ref[...]).astype(BF16)
    rs = lax.rsqrt(jnp.mean(x * x, axis=-1, keepdims=True) + EPS)
    u = u_s[...]
    r_g = (_dot(u, wr_s[...]) * rs).astype(BF16)
    gate = _dot(r_g, wup_ref[...]) + bg_ref[...]
    la_s[...] = _log_sigmoid(gate) * (1.0 / GLA_GATE_TAU)
    c0 = 0
    qg_s[...] = (_dot(u, w16_ref[:, c0:c0 + GLA_KEY_WIDTH]) * (rs * (GLA_DK ** -0.5))).astype(BF16)
    c0 += GLA_KEY_WIDTH
    kg_s[...] = (_dot(u, w16_ref[:, c0:c0 + GLA_KEY_WIDTH]) * rs).astype(BF16)
    c0 += GLA_KEY_WIDTH
    vg_s[...] = (_dot(u, w16_ref[:, c0:c0 + GLA_WIDTH]) * rs).astype(BF16)
    c0 += GLA_WIDTH
    zg_s[...] = _dot(u, w16_ref[:, c0:c0 + GLA_WIDTH]) * rs
    xm_s[HIST:HIST + TS, :] = _dot(u, wml_s[:, 0:MLSTM_WIDTH]) * rs
    zm_s[...] = _dot(u, wml_s[:, MLSTM_WIDTH:2 * MLSTM_WIDTH]) * rs

    front = {}

    def front_conv():
        conv = cb_ref[...] + cw_ref[3:4, :] * xm_s[HIST:HIST + TS, :]
        for w in range(CONV_WIDTH - 1):
            off = HIST - (CONV_WIDTH - 1) + w
            conv = conv + cw_ref[w:w + 1, :] * xm_s[off:off + TS, :]
        cact_s[...] = _silu(conv)

    def front_headwise(j):
        cl = slice(j * 256, (j + 1) * 256)
        c_act = cact_s[:, cl].astype(BF16)
        qm_s[:, cl] = _dot(c_act, bd_ref[0, j]).astype(BF16)
        km_s[:, cl] = _dot(c_act, bd_ref[1, j]).astype(BF16)
        vm_s[:, cl] = _dot(xm_s[HIST:HIST + TS, cl].astype(BF16), bd_ref[2, j]).astype(BF16)

    def front_gates():
        xm_s[0:HIST, :] = xm_s[TS:TS + HIST, :]
        front["g"] = (_dot(qm_s[...], wgate_ref[0:MLSTM_WIDTH, :])
                      + _dot(km_s[...], wgate_ref[MLSTM_WIDTH:2 * MLSTM_WIDTH, :])
                      + _dot(vm_s[...], wgate_ref[2 * MLSTM_WIDTH:3 * MLSTM_WIDTH, :])
                      + bgate_ref[...])

    def front_decay():
        for mc in range(TS // CM):
            g = front["g"][mc * CM:(mc + 1) * CM, :]
            bcum = _cumsum_rows(trim_ref[...], _log_sigmoid(g))
            bcum = pltpu.roll(bcum, LANES - MLSTM_HEADS, axis=1)
            a_all = g - bcum
            front[mc] = (bcum, a_all, a_all.T)

    front_pieces = [front_conv,
                    lambda: (front_headwise(0), front_headwise(1)),
                    lambda: (front_headwise(2), front_headwise(3)),
                    front_gates, front_decay]

    lvl = lvl_ref[...]
    for c in range(TS // CG):
        rows = slice(c * CG, (c + 1) * CG)
        b_s[...] = _cumsum_rows2(trig_ref[...], la_s[rows, :])
        for h in range(GLA_HEADS):
            kl = slice(h * GLA_DK, (h + 1) * GLA_DK)
            vl = slice(h * GLA_DV, (h + 1) * GLA_DV)
            q16 = qg_s[rows, kl]
            k16 = kg_s[rows, kl]
            b = b_s[:, kl]
            e0 = jnp.exp(la_s[rows, kl]).astype(BF16)
            sc = jnp.where(lvl == 0, _dot_nt(q16 * e0, k16), 0.0)
            sc = jnp.where(lvl == GLA_LEVELS, _dot_nt(q16, k16), sc)
            for p in range(1, GLA_LEVELS):
                e = jnp.exp2((b - _midpoint_rows(b_s, kl, p)) * escale_ref[p]).astype(BF16)
                sc = jnp.where(lvl == p, _dot_nt(q16 * e, k16 * e), sc)
            v = vg_s[rows, vl]
            st = sg_s[h]
            o = (_dot(sc.astype(BF16), v)
                 + _dot_nt(q16 * jnp.exp(b).astype(BF16), st.astype(BF16)))
            b_last = b[CG - 1:CG, :]
            k_dec = k16 * jnp.exp(b_last - b).astype(BF16)
            sg_s[h] = st * jnp.exp(b_last) + _dot_tn(v, k_dec)
            o = o * lax.rsqrt(jnp.mean(o * o, axis=-1, keepdims=True) + EPS) * glag_ref[:, vl]
            mix_s[rows, vl] = (o * _silu(zg_s[rows, vl])).astype(BF16)
            if front_pieces:
                front_pieces.pop(0)()

    assert not front_pieces
    causal = (lax.broadcasted_iota(jnp.int32, (CM, CM), 0)
              >= lax.broadcasted_iota(jnp.int32, (CM, CM), 1))
    log_kscale = 0.5 * math.log(MLSTM_DH)
    for mc in range(TS // CM):
        mrows = slice(mc * CM, (mc + 1) * CM)
        bcum, a_all, a_all_t = front[mc]
        for h in range(MLSTM_HEADS):
            hl = slice(h * MLSTM_DH, (h + 1) * MLSTM_DH)
            a_row = a_all_t[h:h + 1, :]
            a_col = a_all[:, h:h + 1]
            b_col = bcum[:, h:h + 1]
            m_prev = mm_s[h][0:1, 0:1]
            run_max = jnp.max(jnp.where(causal, a_row, -jnp.inf), axis=-1, keepdims=True)
            m_rel = jnp.maximum(m_prev, run_max)
            dmat = jnp.exp(jnp.where(causal, (a_row - log_kscale) - m_rel, -jnp.inf))
            q = qm_s[mrows, hl]
            k = km_s[mrows, hl]
            v = vm_s[mrows, hl]
            s = _dot_nt(q, k) * dmat
            w_inter = jnp.exp(m_prev - m_rel)
            ct = cm_s[h]
            n_row = nm_s[h][0:1, :]
            num = _dot(s.astype(BF16), v) + w_inter * _dot(q, ct.astype(BF16))
            den = (jnp.sum(s, axis=-1, keepdims=True)
                   + w_inter * jnp.sum(q.astype(F32) * n_row, axis=-1, keepdims=True))
            m_abs = b_col + m_rel
            hh = num * (1.0 / jnp.maximum(jnp.abs(den), jnp.exp(-m_abs)))
            hh = hh * lax.rsqrt(jnp.mean(hh * hh, axis=-1, keepdims=True) + EPS) * mng_ref[:, hl]
            o = (hh + skip_ref[:, hl] * cact_s[mrows, hl]) * _silu(zm_s[mrows, hl])
            mix_s[mrows, GLA_WIDTH + h * MLSTM_DH:GLA_WIDTH + (h + 1) * MLSTM_DH] = o.astype(BF16)
            m_last = m_rel[CM - 1:CM, :]
            decay = jnp.exp(m_prev - m_last)
            wk_col = jnp.exp((a_col - log_kscale) - m_last)
            wk_row = jnp.exp((a_row - log_kscale) - m_last)
            vw = (v.astype(F32) * wk_col).astype(BF16)
            cm_s[h] = decay * ct + _dot_tn(k, vw)
            n_add = _dot(jnp.broadcast_to(wk_row, (SUBLANES, CM)).astype(BF16), k)
            nm_s[h] = decay * nm_s[h] + n_add
            mm_s[h] = jnp.broadcast_to(b_col[CM - 1:CM, :] + m_last, (SUBLANES, LANES))

    for r0 in range(0, TS, CM):
        y = x_ref[pl.ds(row0 + r0, CM), :] + _dot(mix_s[r0:r0 + CM, :], wout_ref[...])
        out_ref[pl.ds(row0 + r0, CM), :] = (y * lax.rsqrt(jnp.mean(y * y, axis=-1, keepdims=True) + EPS)
                                            * fng_ref[...])


def _block_diag_tiles(ws):
    wf = jnp.stack(ws).reshape(-1, QKV_BLOCK).astype(BF16)
    sel = (jnp.arange(256)[None, :] % QKV_BLOCK == jnp.arange(QKV_BLOCK)[:, None]).astype(BF16)
    spread = jnp.dot(wf, sel, preferred_element_type=F32)
    r = (jnp.arange(wf.shape[0])[:, None] % 256) // QKV_BLOCK
    c = jnp.arange(256)[None, :] // QKV_BLOCK
    return jnp.where(r == c, spread, 0.0).astype(BF16).reshape(len(ws), -1, 256, 256)


def kernel(x, norm_g, w_in, w_gla_gate_up, b_gla_gate, gla_norm_g, conv_w, conv_b, w_q_m, w_k_m, w_v_m, w_igate, b_igate, w_fgate, b_fgate, mlstm_norm_g, mlstm_skip, w_out, final_norm_g):
    bsz, seq, d = x.shape
    assert d == D_MODEL and seq % TB == 0 and TB % TS == 0

    w_up = jnp.pad(w_gla_gate_up, ((0, LANES - GLA_GATE_RANK), (0, 0))).astype(BF16)
    bd = _block_diag_tiles([w_q_m, w_k_m, w_v_m])
    w_gate = jnp.pad(jnp.concatenate([w_igate, w_fgate], axis=1),
                     ((0, 0), (0, LANES - 2 * MLSTM_HEADS))).astype(BF16)
    b_gate = jnp.pad(jnp.concatenate([b_igate, b_fgate]), (0, LANES - 2 * MLSTM_HEADS)).reshape(1, LANES)
    conv_w8 = jnp.pad(conv_w, ((0, SUBLANES - CONV_WIDTH), (0, 0)))
    row = lambda a: a.reshape(1, -1)

    operands = [
        x, row(norm_g), w_in.astype(BF16), w_up, row(b_gla_gate), row(gla_norm_g), conv_w8, row(conv_b),
        bd, w_gate, b_gate, row(mlstm_norm_g), row(mlstm_skip), w_out.astype(BF16),
        row(final_norm_g),
        _tri(CG), _pair_level(CG, GLA_LEVELS), _level_exp2_scale(CG, GLA_LEVELS), _tri(CM),
    ]

    def const_spec(a):
        nd = a.ndim
        return pl.BlockSpec(a.shape, lambda b, t, _nd=nd: (0,) * _nd, pipeline_mode=pl.Buffered(1))

    tok_spec = pl.BlockSpec((None, TB, D_MODEL), lambda b, t: (b, t, 0))
    in_specs = [tok_spec] + [const_spec(a) for a in operands[1:]]

    scratch = [
        pltpu.VMEM((TS, D_MODEL), BF16),
        pltpu.VMEM((TS, GLA_KEY_WIDTH), BF16),
        pltpu.VMEM((TS, GLA_KEY_WIDTH), BF16),
        pltpu.VMEM((TS, GLA_WIDTH), BF16),
        pltpu.VMEM((TS, GLA_WIDTH), F32),
        pltpu.VMEM((TS, GLA_KEY_WIDTH), F32),
        pltpu.VMEM((CG, GLA_KEY_WIDTH), F32),
        pltpu.VMEM((HIST + TS, MLSTM_WIDTH), F32),
        pltpu.VMEM((TS, MLSTM_WIDTH), F32),
        pltpu.VMEM((TS, MLSTM_WIDTH), F32),
        pltpu.VMEM((TS, MLSTM_WIDTH), BF16),
        pltpu.VMEM((TS, MLSTM_WIDTH), BF16),
        pltpu.VMEM((TS, MLSTM_WIDTH), BF16),
        pltpu.VMEM((TS, GLA_WIDTH + MLSTM_WIDTH), BF16),
        pltpu.VMEM((GLA_HEADS, GLA_DV, GLA_DK), F32),
        pltpu.VMEM((MLSTM_HEADS, MLSTM_DH, MLSTM_DH), F32),
        pltpu.VMEM((MLSTM_HEADS, SUBLANES, MLSTM_DH), F32),
        pltpu.VMEM((MLSTM_HEADS, SUBLANES, LANES), F32),
        pltpu.VMEM((D_MODEL, 2 * MLSTM_WIDTH), BF16),
        pltpu.VMEM((D_MODEL, LANES), BF16),
    ]

    return pl.pallas_call(
        _layer_kernel,
        out_shape=jax.ShapeDtypeStruct(x.shape, x.dtype),
        grid=(bsz, seq // TB),
        in_specs=in_specs,
        out_specs=tok_spec,
        scratch_shapes=scratch,
        compiler_params=pltpu.CompilerParams(
            dimension_semantics=("arbitrary", "arbitrary"),
            vmem_limit_bytes=VMEM_LIMIT_BYTES),
        name="gla_mlstm_layer",
    )(*operands)
```
